```python
import math
import jax
import jax.numpy as jnp
from jax import lax
import numpy as np

D_MODEL = 1024
BATCH = 32
SEQ = 256
DEPTH = 4
DEC_BATCH = 8
DEC_SEQ = 4096
PAST_LEN = 512

GRID_W = 64
HEAD_DIM = 64
BLOCK = 128
N_AB = (DEPTH + 1) // 2
N_C = DEPTH // 2
A_HEADS = 8
A_KV_HEADS = 2
A_GROUP = A_HEADS // A_KV_HEADS
WINDOW = 128
A_COLS = (A_HEADS + 2 * A_KV_HEADS) * HEAD_DIM
B_HEADS = 8
B_WIDTH = B_HEADS * HEAD_DIM
DECAY_LORA = 64
AAA_LORA = 64
GATE_LORA = 128
B_COLS = 3 * B_WIDTH + 2 * DECAY_LORA + 2 * AAA_LORA + GATE_LORA
AB_COLS = A_COLS + B_COLS
C_HEADS = 8
C_VDIM = 2 * HEAD_DIM
C_COLS = 3 * C_HEADS * C_VDIM
D_FF = 2816
N_MOD = 9
ROPE_THETA = 10000.0
NORM_EPS = 1e-6
GN_EPS = 64e-5
SUBLN_EPS = 1e-5

kernel_name = "hybrid_diffusion_trunk_step"


def _split(x, sizes):
    idx = np.cumsum(sizes)[:-1].tolist()
    return jnp.split(x, idx, axis=-1)


def _rmsnorm(x, g):
    xf = x.astype(jnp.float32)
    y = xf * lax.rsqrt(jnp.mean(xf * xf, axis=-1, keepdims=True) + NORM_EPS)
    return (y * g.astype(jnp.float32)).astype(x.dtype)


def _swiglu(h, w1, w3, w2):
    return (jax.nn.silu(h @ w1) * (h @ w3)) @ w2


def _axial_rope(n_tok):
    n_rows = n_tok // GRID_W
    row = jnp.repeat(jnp.arange(n_rows, dtype=jnp.float32), GRID_W)
    col = jnp.tile(jnp.arange(GRID_W, dtype=jnp.float32), n_rows)
    n_freq = HEAD_DIM // 4
    inv = jnp.power(ROPE_THETA, -jnp.arange(n_freq, dtype=jnp.float32) / n_freq)
    ang_r = (row[:, None] * inv)[:, None, :]
    ang_c = (col[:, None] * inv)[:, None, :]
    return (jnp.cos(ang_r), jnp.sin(ang_r), jnp.cos(ang_c), jnp.sin(ang_c))


def _rotate(x, cos, sin):
    x1, x2 = jnp.split(x, 2, axis=-1)
    return jnp.concatenate([x1 * cos - x2 * sin, x2 * cos + x1 * sin], axis=-1)


def _apply_axial_rope(x, rope):
    cos_r, sin_r, cos_c, sin_c = rope
    x_row, x_col = jnp.split(x.astype(jnp.float32), 2, axis=-1)
    return jnp.concatenate([_rotate(x_row, cos_r, sin_r), _rotate(x_col, cos_c, sin_c)], axis=-1).astype(x.dtype)


def _rope_pairs(x, rope):
    b, t, h, _ = x.shape
    return _apply_axial_rope(x.reshape(b, t, 2 * h, HEAD_DIM), rope).reshape(b, t, h, 2 * HEAD_DIM)


def _gqa_blocks(q, k_ctx, v_ctx, sink, k_lat=None, v_lat=None):
    bn, t = q.shape[0], q.shape[1]
    n_ctx = k_ctx.shape[1]
    nb = t // BLOCK
    scale = HEAD_DIM ** -0.5
    qb = q.reshape(bn, nb, BLOCK, A_KV_HEADS, A_GROUP, HEAD_DIM).swapaxes(0, 1)
    sink_s = jnp.broadcast_to(sink.astype(jnp.float32).reshape(1, A_KV_HEADS, A_GROUP, 1, 1),
                              (bn, A_KV_HEADS, A_GROUP, BLOCK, 1))
    windowed = k_lat is not None
    if windowed:
        pad = ((0, 0), (BLOCK, BLOCK), (0, 0), (0, 0))
        kp, vp = jnp.pad(k_lat, pad), jnp.pad(v_lat, pad)
        rel = jnp.arange(3 * BLOCK)[None, :] - BLOCK - jnp.arange(BLOCK)[:, None]
        band = jnp.abs(rel) <= WINDOW

    def one_block(args):
        qi, i = args
        s_ctx = jnp.einsum("bqkgd,blkd->bkgql", qi, k_ctx).astype(jnp.float32) * scale
        parts = [s_ctx, sink_s]
        if windowed:
            start = i * BLOCK
            kw = lax.dynamic_slice_in_dim(kp, start, 3 * BLOCK, axis=1)
            vw = lax.dynamic_slice_in_dim(vp, start, 3 * BLOCK, axis=1)
            kpos = start - BLOCK + jnp.arange(3 * BLOCK)
            valid = band & ((kpos >= 0) & (kpos < t))[None, :]
            s_win = jnp.einsum("bqkgd,blkd->bkgql", qi, kw).astype(jnp.float32) * scale
            parts = [jnp.where(valid, s_win, -jnp.inf)] + parts
        p = jax.nn.softmax(jnp.concatenate(parts, axis=-1), axis=-1)
        if windowed:
            p_win = p[..., :3 * BLOCK].astype(vw.dtype)
            p_ctx = p[..., 3 * BLOCK:3 * BLOCK + n_ctx].astype(v_ctx.dtype)
            return (jnp.einsum("bkgql,blkd->bqkgd", p_win, vw)
                    + jnp.einsum("bkgql,blkd->bqkgd", p_ctx, v_ctx))
        return jnp.einsum("bkgql,blkd->bqkgd", p[..., :n_ctx].astype(v_ctx.dtype), v_ctx)

    out = lax.map(one_block, (qb, jnp.arange(nb)))
    return out.swapaxes(0, 1).reshape(bn, t, A_HEADS * HEAD_DIM)


def _token_shift(p, mu_prev, mu_next):
    prev = jnp.pad(p, ((0, 0), (1, 0), (0, 0)))[:, :-1]
    nxt = jnp.pad(p, ((0, 0), (0, 1), (0, 0)))[:, 1:]
    return p + mu_prev * (prev - p) + mu_next * (nxt - p)


def _wkv_scan(r, w, k, v, kk, a, s0, reverse):
    xs = tuple(z.astype(jnp.float32).transpose(1, 0, 2, 3) for z in (r, w, k, v, kk, a))

    def step(s, inp):
        r_t, w_t, k_t, v_t, kk_t, a_t = inp
        sa = jnp.einsum("bhij,bhj->bhi", s, kk_t)
        s = (s * w_t[:, :, None, :] - sa[..., None] * (kk_t * a_t)[:, :, None, :]
             + v_t[..., None] * k_t[:, :, None, :])
        return s, jnp.einsum("bhij,bhj->bhi", s, r_t)

    s_fin, ys = lax.scan(step, s0.astype(jnp.float32), xs, reverse=reverse)
    return ys.transpose(1, 0, 2, 3), s_fin


def _rwkv_mixer(pb, s0, j, W):
    bn, t, _ = pb.shape
    r, k, v, wf, wb, af, ab, gd = _split(
        pb, (B_WIDTH, B_WIDTH, B_WIDTH, DECAY_LORA, DECAY_LORA, AAA_LORA, AAA_LORA, GATE_LORA))
    hs = lambda z: z.astype(jnp.float32).reshape(bn, t, B_HEADS, HEAD_DIM)
    kk = hs(k * W["b_k_k"][j])
    kk = kk / jnp.maximum(jnp.sqrt(jnp.sum(kk * kk, axis=-1, keepdims=True)), 1e-12)
    rh, vh = hs(r), hs(v)
    ys, bonus, finals = [], [], []
    for d, (w_dn, a_dn) in enumerate(((wf, af), (wb, ab))):
        w_log = -jax.nn.softplus(-(W["b_w0"][j, d] + jnp.tanh(w_dn) @ W["b_w2"][j, d])) - 0.5
        decay = jnp.exp(-jnp.exp(w_log.astype(jnp.float32)))
        a = jax.nn.sigmoid(W["b_a0"][j, d] + a_dn @ W["b_a2"][j, d])
        k_d = hs(k * (1 + (a - 1) * W["b_k_a"][j]))
        y_d, s_d = _wkv_scan(rh, hs(decay), k_d, vh, kk, hs(a), s0[:, d], reverse=(d == 1))
        ys.append(y_d)
        bonus.append(jnp.sum(rh * k_d * W["b_r_k"][j].astype(jnp.float32), axis=-1, keepdims=True) * vh)
        finals.append(s_d)
    y = ys[0] + ys[1]
    mu = jnp.mean(y, axis=-1, keepdims=True)
    var = jnp.mean(jnp.square(y - mu), axis=-1, keepdims=True)
    yn = ((y - mu) * lax.rsqrt(var + GN_EPS)).reshape(bn, t, B_WIDTH) * W["b_gn_w"][j] + W["b_gn_b"][j]
    yn = yn + (bonus[0] + bonus[1]).reshape(bn, t, B_WIDTH)
    g = jax.nn.sigmoid(gd) @ W["b_g2"][j]
    return (yn * g).astype(pb.dtype), jnp.stack(finals, axis=1)


def _mixer_ab(h, j, W, rope, cache):
    bn, t, _ = h.shape
    p = h @ W["w_in_ab"][j]
    qa, ka, va = _split(p[..., :A_COLS], (A_HEADS * HEAD_DIM, A_KV_HEADS * HEAD_DIM, A_KV_HEADS * HEAD_DIM))
    qa = qa.reshape(bn, t, A_HEADS, HEAD_DIM)
    ka = ka.reshape(bn, t, A_KV_HEADS, HEAD_DIM)
    va = va.reshape(bn, t, A_KV_HEADS, HEAD_DIM)
    pb = _token_shift(p[..., A_COLS:], W["b_mu_prev"][j], W["b_mu_next"][j])
    sink = W["a_sink"][j]
    if cache is None:
        oa = _gqa_blocks(qa, ka, va, sink)
        s0 = jnp.zeros((bn, 2, B_HEADS, HEAD_DIM, HEAD_DIM), jnp.float32)
    else:
        k_ctx, v_ctx, s0 = cache
        oa = _gqa_blocks(_apply_axial_rope(qa, rope), k_ctx, v_ctx, sink,
                         _apply_axial_rope(ka, rope), va)
    ob, s_fin = _rwkv_mixer(pb, s0, j, W)
    out = jnp.concatenate([oa, ob], axis=-1) @ W["w_out_ab"][j]
    return out, (ka, va, s_fin)


def _diff_attn(q, k, v, lam):
    bn, t = q.shape[0], q.shape[1]
    nb = t // BLOCK
    scale = HEAD_DIM ** -0.5
    qb = q.reshape(bn, nb, BLOCK, C_HEADS, 2, HEAD_DIM).swapaxes(0, 1)
    k2 = k.reshape(bn, k.shape[1], C_HEADS, 2, HEAD_DIM)

    def one_block(qi):
        s = jnp.einsum("bqhmd,bnhmd->bhmqn", qi, k2).astype(jnp.float32) * scale
        p = jax.nn.softmax(s, axis=-1)
        amap = p[:, :, 0] - lam * p[:, :, 1]
        return jnp.einsum("bhqn,bnhe->bqhe", amap.astype(v.dtype), v)

    o = lax.map(one_block, qb)
    return o.swapaxes(0, 1).reshape(bn, t, C_HEADS, C_VDIM)


def _mixer_c(h, j, l, W, rope, cache):
    bn, t, _ = h.shape
    q, k, v = jnp.split(h @ W["w_in_c"][j], 3, axis=-1)
    shp = (bn, t, C_HEADS, 2 * HEAD_DIM)
    q, k, v = q.reshape(shp), k.reshape(shp), v.reshape(shp)
    lam_init = 0.8 - 0.6 * math.exp(-0.3 * l)
    lam = (jnp.exp(jnp.sum(W["c_lq1"][j] * W["c_lk1"][j]).astype(jnp.float32))
           - jnp.exp(jnp.sum(W["c_lq2"][j] * W["c_lk2"][j]).astype(jnp.float32)) + lam_init)
    if cache is None:
        o = _diff_attn(q, k, v, lam)
    else:
        k_ctx, v_ctx = cache
        keys = jnp.concatenate([k_ctx, _rope_pairs(k, rope)], axis=1)
        vals = jnp.concatenate([v_ctx, v], axis=1)
        o = _diff_attn(_rope_pairs(q, rope), keys, vals, lam)
    of = o.astype(jnp.float32)
    of = of * lax.rsqrt(jnp.mean(of * of, axis=-1, keepdims=True) + SUBLN_EPS) * W["c_subln"][j] * (1.0 - lam_init)
    out = of.astype(h.dtype).reshape(bn, t, C_HEADS * C_VDIM) @ W["w_out_c"][j]
    return out, (k, v)


def _forward(x, cond, caches, W, latent):
    t = x.shape[1]
    rope = _axial_rope(t) if latent else None
    ak, av, bs, ck, cv = [], [], [], [], []
    for l in range(DEPTH):
        mod = (jax.nn.silu(cond) @ W["mod_w"][l] + W["mod_b"][l]).reshape(-1, 1, N_MOD * D_MODEL)
        sh1, sc1, g1, sh2, sc2, g2, sh3, sc3, g3 = jnp.split(mod, N_MOD, axis=-1)
        h = _rmsnorm(x, W["norms"][l, 0]) * (1 + sc1) + sh1
        x = x + 0.5 * g1 * _swiglu(h, W["ffn_w1"][l, 0], W["ffn_w3"][l, 0], W["ffn_w2"][l, 0])
        h = _rmsnorm(x, W["norms"][l, 1]) * (1 + sc2) + sh2
        j = l // 2
        if l % 2 == 0:
            cache = (caches["a_k"][:, j], caches["a_v"][:, j], caches["b_s"][:, j]) if latent else None
            m, st = _mixer_ab(h, j, W, rope, cache)
            if not latent:
                ak.append(st[0]); av.append(st[1]); bs.append(st[2])
        else:
            cache = (caches["c_k"][:, j], caches["c_v"][:, j]) if latent else None
            m, st = _mixer_c(h, j, l, W, rope, cache)
            if not latent:
                ck.append(st[0]); cv.append(st[1])
        x = x + g2 * m.astype(x.dtype)
        h = _rmsnorm(x, W["norms"][l, 2]) * (1 + sc3) + sh3
        x = x + 0.5 * g3 * _swiglu(h, W["ffn_w1"][l, 1], W["ffn_w3"][l, 1], W["ffn_w2"][l, 1])
    return _rmsnorm(x, W["final_norm"]), (ak, av, bs, ck, cv)


def setup_inputs(seed: int = 0) -> dict:
    key = jax.random.key(seed)
    ks = iter(jax.random.split(key, 48))
    nrm = lambda shape, s=1.0: s * jax.random.normal(next(ks), shape, jnp.float32)
    uni = lambda shape, lo, hi: jax.random.uniform(next(ks), shape, jnp.float32, lo, hi)
    D = D_MODEL
    return {
        "x_prompt": nrm((BATCH, SEQ, D)),
        "x_sample": nrm((DEC_BATCH, DEC_SEQ, D)),
        "c": nrm((DEC_BATCH, D)),
        "c_ctx": nrm((D,)),
        "cache_a_k": nrm((DEC_BATCH, N_AB, PAST_LEN, A_KV_HEADS, HEAD_DIM)),
        "cache_a_v": nrm((DEC_BATCH, N_AB, PAST_LEN, A_KV_HEADS, HEAD_DIM)),
        "state_b_wkv": nrm((DEC_BATCH, N_AB, 2, B_HEADS, HEAD_DIM, HEAD_DIM), 0.5),
        "cache_c_k": nrm((DEC_BATCH, N_C, PAST_LEN, C_HEADS, 2 * HEAD_DIM)),
        "cache_c_v": nrm((DEC_BATCH, N_C, PAST_LEN, C_HEADS, C_VDIM)),
        "norms": 1.0 + nrm((DEPTH, 3, D), 0.02),
        "mod_w": nrm((DEPTH, D, N_MOD * D), 0.02),
        "mod_b": nrm((DEPTH, N_MOD * D), 0.02),
        "ffn_w1": nrm((DEPTH, 2, D, D_FF), D ** -0.5),
        "ffn_w3": nrm((DEPTH, 2, D, D_FF), D ** -0.5),
        "ffn_w2": nrm((DEPTH, 2, D_FF, D), D_FF ** -0.5),
        "w_in_ab": nrm((N_AB, D, AB_COLS), D ** -0.5),
        "w_out_ab": nrm((N_AB, A_HEADS * HEAD_DIM + B_WIDTH, D), (A_HEADS * HEAD_DIM + B_WIDTH) ** -0.5),
        "a_sink": nrm((N_AB, A_HEADS), 0.5),
        "b_mu_prev": uni((N_AB, B_COLS), 0.0, 0.5),
        "b_mu_next": uni((N_AB, B_COLS), 0.0, 0.5),
        "b_w0": uni((N_AB, 2, B_WIDTH), -5.0, 0.0),
        "b_w2": nrm((N_AB, 2, DECAY_LORA, B_WIDTH), 0.1),
        "b_a0": nrm((N_AB, 2, B_WIDTH), 0.5),
        "b_a2": nrm((N_AB, 2, AAA_LORA, B_WIDTH), 0.1),
        "b_k_k": 0.85 + nrm((N_AB, B_WIDTH), 0.05),
        "b_k_a": 1.0 + nrm((N_AB, B_WIDTH), 0.05),
        "b_r_k": nrm((N_AB, B_HEADS, HEAD_DIM), 0.1),
        "b_g2": nrm((N_AB, GATE_LORA, B_WIDTH), GATE_LORA ** -0.5),
        "b_gn_w": 1.0 + nrm((N_AB, B_WIDTH), 0.02),
        "b_gn_b": nrm((N_AB, B_WIDTH), 0.02),
        "w_in_c": nrm((N_C, D, C_COLS), D ** -0.5),
        "w_out_c": nrm((N_C, C_HEADS * C_VDIM, D), (C_HEADS * C_VDIM) ** -0.5),
        "c_lq1": nrm((N_C, HEAD_DIM), 0.1),
        "c_lk1": nrm((N_C, HEAD_DIM), 0.1),
        "c_lq2": nrm((N_C, HEAD_DIM), 0.1),
        "c_lk2": nrm((N_C, HEAD_DIM), 0.1),
        "c_subln": 1.0 + nrm((N_C, C_VDIM), 0.02),
        "final_norm": 1.0 + nrm((D,), 0.02),
    }


def reference(x_prompt, x_sample, c, c_ctx, cache_a_k, cache_a_v, state_b_wkv, cache_c_k, cache_c_v,
              norms, mod_w, mod_b, ffn_w1, ffn_w3, ffn_w2,
              w_in_ab, w_out_ab, a_sink, b_mu_prev, b_mu_next, b_w0, b_w2, b_a0, b_a2,
              b_k_k, b_k_a, b_r_k, b_g2, b_gn_w, b_gn_b,
              w_in_c, w_out_c, c_lq1, c_lk1, c_lq2, c_lk2, c_subln, final_norm):
    W = dict(norms=norms, mod_w=mod_w, mod_b=mod_b, ffn_w1=ffn_w1, ffn_w3=ffn_w3, ffn_w2=ffn_w2,
             w_in_ab=w_in_ab, w_out_ab=w_out_ab, a_sink=a_sink, b_mu_prev=b_mu_prev, b_mu_next=b_mu_next,
             b_w0=b_w0, b_w2=b_w2, b_a0=b_a0, b_a2=b_a2, b_k_k=b_k_k, b_k_a=b_k_a, b_r_k=b_r_k,
             b_g2=b_g2, b_gn_w=b_gn_w, b_gn_b=b_gn_b, w_in_c=w_in_c, w_out_c=w_out_c,
             c_lq1=c_lq1, c_lk1=c_lk1, c_lq2=c_lq2, c_lk2=c_lk2, c_subln=c_subln, final_norm=final_norm)
    y_prompt, (ak, av, bs, ck, cv) = _forward(x_prompt, c_ctx, None, W, latent=False)
    caches = dict(a_k=cache_a_k, a_v=cache_a_v, b_s=state_b_wkv, c_k=cache_c_k, c_v=cache_c_v)
    y_sample, _ = _forward(x_sample, c, caches, W, latent=True)
    new_a_k = jnp.stack(ak, axis=1)
    new_a_v = jnp.stack(av, axis=1)
    new_b_wkv = jnp.stack(bs, axis=1)
    new_c_k = jnp.stack(ck, axis=1)
    new_c_v = jnp.stack(cv, axis=1)
    return (y_prompt, y_sample, new_a_k, new_a_v, new_b_wkv, new_c_k, new_c_v)
```

```python
import functools
import math

import numpy as np
import jax
import jax.numpy as jnp
from jax import lax
from jax.experimental import pallas as pl
from jax.experimental.pallas import tpu as pltpu

F32 = jnp.float32
BF16 = jnp.bfloat16
HIGHEST = lax.Precision.HIGHEST

D_MODEL = 1024
DEPTH = 4
GRID_W = 64
HEAD_DIM = 64
BLOCK = 128
A_HEADS = 8
A_KV_HEADS = 2
A_GROUP = A_HEADS // A_KV_HEADS
A_Q = A_HEADS * HEAD_DIM
A_KV = A_KV_HEADS * HEAD_DIM
A_COLS = A_Q + 2 * A_KV
B_HEADS = 8
B_WIDTH = B_HEADS * HEAD_DIM
DECAY_LORA = 64
AAA_LORA = 64
GATE_LORA = 128
B_COLS = 3 * B_WIDTH + 2 * DECAY_LORA + 2 * AAA_LORA + GATE_LORA
C_HEADS = 8
C_VDIM = 2 * HEAD_DIM
C_WIDTH = C_HEADS * C_VDIM
D_FF = 2816
N_MOD = 9
ROPE_THETA = 10000.0
NORM_EPS = 1e-6
GN_EPS = 64e-5
SUBLN_EPS = 1e-5

LANES = 128
MOD_ROWS = 16
CHUNK = 64
PAIRS = B_HEADS // 2
VMEM_LIMIT = 56 * 2 ** 20


def _params(*sem):
    return pltpu.CompilerParams(dimension_semantics=sem, vmem_limit_bytes=VMEM_LIMIT)


def _dot(a, b, precision=None):
    return jnp.dot(a, b, preferred_element_type=F32, precision=precision)


def _dot_nt(a, b, precision=None):
    return lax.dot_general(a, b, (((1,), (1,)), ((), ())), preferred_element_type=F32, precision=precision)


def _dot_tn(a, b, precision=None):
    return lax.dot_general(a, b, (((0,), (0,)), ((), ())), preferred_element_type=F32, precision=precision)


def _silu(x):
    return x * jax.nn.sigmoid(x)


def _modnorm(x, g, scale, shift):
    y = x * lax.rsqrt(jnp.mean(x * x, axis=-1, keepdims=True) + NORM_EPS)
    return (y * g) * (1.0 + scale) + shift


def _mod_kernel(c_ref, w_ref, b_ref, o_ref):
    o_ref[...] = _dot(_silu(c_ref[...]), w_ref[...], HIGHEST) + b_ref[...]


def _modulation(cond, mod_w, mod_b):
    n_col = N_MOD * D_MODEL
    tn = D_MODEL
    out = pl.pallas_call(
        _mod_kernel,
        grid=(DEPTH, n_col // tn),
        in_specs=[pl.BlockSpec((MOD_ROWS, D_MODEL), lambda l, j: (0, 0)),
                  pl.BlockSpec((None, D_MODEL, tn), lambda l, j: (l, 0, j)),
                  pl.BlockSpec((None, 1, tn), lambda l, j: (l, 0, j))],
        out_specs=pl.BlockSpec((None, MOD_ROWS, tn), lambda l, j: (l, 0, j)),
        out_shape=jax.ShapeDtypeStruct((DEPTH, MOD_ROWS, n_col), F32),
        compiler_params=_params("parallel", "parallel"),
        name="modulation",
    )(cond, mod_w, mod_b.reshape(DEPTH, 1, n_col))
    return out.reshape(DEPTH, MOD_ROWS, N_MOD, D_MODEL)


def _mod_spec(tm, seq_len, row0):
    if row0 == 0:
        return pl.BlockSpec((None, N_MOD, D_MODEL), lambda i: (0, 0, 0))
    return pl.BlockSpec((None, N_MOD, D_MODEL), lambda i: (row0 + (i * tm) // seq_len, 0, 0))


def _resident(shape):
    nd = len(shape)
    return pl.BlockSpec(shape, lambda i: (0,) * nd)


FF_TILE = 256


def _ffn_kernel(x_ref, mod_ref, g_ref, w1_ref, w3_ref, w2_ref, gf_ref, o_ref, act_ref, *, sub, final):
    x = x_ref[...]
    h = _modnorm(x, g_ref[...], mod_ref[3 * sub + 1:3 * sub + 2, :], mod_ref[3 * sub:3 * sub + 1, :]).astype(BF16)
    for j in range(D_FF // FF_TILE):
        cols = slice(j * FF_TILE, (j + 1) * FF_TILE)
        a = _dot(h, w1_ref[:, cols])
        b = _dot(h, w3_ref[:, cols])
        act_ref[:, cols] = (_silu(a) * b).astype(BF16)
    y = x + 0.5 * mod_ref[3 * sub + 2:3 * sub + 3, :] * _dot(act_ref[...], w2_ref[...])
    if final:
        y = y * lax.rsqrt(jnp.mean(y * y, axis=-1, keepdims=True) + NORM_EPS) * gf_ref[...]
    o_ref[...] = y


def _ffn(x, mod_l, g, w1, w3, w2, gf, *, sub, seq_len, row0, final=False):
    n = x.shape[0]
    tm = min(512, seq_len)
    return pl.pallas_call(
        functools.partial(_ffn_kernel, sub=sub, final=final),
        grid=(n // tm,),
        in_specs=[pl.BlockSpec((tm, D_MODEL), lambda i: (i, 0)),
                  _mod_spec(tm, seq_len, row0),
                  _resident((1, D_MODEL)),
                  _resident((D_MODEL, D_FF)),
                  _resident((D_MODEL, D_FF)),
                  _resident((D_FF, D_MODEL)),
                  _resident((1, D_MODEL))],
        out_specs=pl.BlockSpec((tm, D_MODEL), lambda i: (i, 0)),
        out_shape=jax.ShapeDtypeStruct((n, D_MODEL), F32),
        scratch_shapes=[pltpu.VMEM((tm, D_FF), BF16)],
        compiler_params=_params("parallel"),
        name="ffn",
    )(x, mod_l, g, w1, w3, w2, gf)


def _swap16(x):
    lane = lax.broadcasted_iota(jnp.int32, x.shape, 1)
    return jnp.where(lane % 32 < 16, pltpu.roll(x, LANES - 16, axis=1), pltpu.roll(x, 16, axis=1))


def _proj_kernel(*refs, sub, splits, n_rope):
    if n_rope:
        x_ref, mod_ref, g_ref, w_ref, cos_ref, sin_ref = refs[:6]
        o_refs = refs[6:]
    else:
        x_ref, mod_ref, g_ref, w_ref = refs[:4]
        o_refs = refs[4:]
    h = _modnorm(x_ref[...], g_ref[...], mod_ref[3 * sub + 1:3 * sub + 2, :], mod_ref[3 * sub:3 * sub + 1, :]).astype(BF16)
    start = 0
    for o_ref, width in zip(o_refs, splits):
        for c in range(0, width, 256):
            wd = min(256, width - c)
            y = _dot(h, w_ref[:, start + c:start + c + wd])
            for p in range(0, wd, LANES):
                yp = y[:, p:p + LANES]
                if start + c + p < n_rope:
                    yp = yp * cos_ref[...] + _swap16(yp) * sin_ref[...]
                o_ref[:, c + p:c + p + LANES] = yp.astype(o_ref.dtype)
        start += width


def _proj(x, mod_l, g, w, rope, *, sub, splits, dtypes, n_rope, seq_len, row0):
    n = x.shape[0]
    tm = min(512, seq_len)
    n_col = w.shape[1]
    in_specs = [pl.BlockSpec((tm, D_MODEL), lambda i: (i, 0)),
                _mod_spec(tm, seq_len, row0),
                _resident((1, D_MODEL)),
                _resident((D_MODEL, n_col))]
    args = [x, mod_l, g, w]
    if n_rope:
        tiles = seq_len // tm
        in_specs += [pl.BlockSpec((tm, LANES), lambda i: (i % tiles, 0))] * 2
        args += list(rope)
    return pl.pallas_call(
        functools.partial(_proj_kernel, sub=sub, splits=splits, n_rope=n_rope),
        grid=(n // tm,),
        in_specs=in_specs,
        out_specs=[pl.BlockSpec((tm, wd), lambda i: (i, 0)) for wd in splits],
        out_shape=[jax.ShapeDtypeStruct((n, wd), dt) for wd, dt in zip(splits, dtypes)],
        compiler_params=_params("parallel"),
        name="mixer_in_proj",
    )(*args)


def _rope_tables(n_tok):
    t = jnp.arange(n_tok, dtype=jnp.int32)
    row = (t // GRID_W).astype(F32)
    col = (t % GRID_W).astype(F32)
    n_freq = HEAD_DIM // 4
    inv = jnp.power(ROPE_THETA, -jnp.arange(n_freq, dtype=F32) / n_freq)
    ar, ac = row[:, None] * inv, col[:, None] * inv
    cos = jnp.concatenate([jnp.cos(ar), jnp.cos(ar), jnp.cos(ac), jnp.cos(ac)], axis=1)
    sin = jnp.concatenate([-jnp.sin(ar), jnp.sin(ar), -jnp.sin(ac), jnp.sin(ac)], axis=1)
    return jnp.tile(cos, (1, 2)), jnp.tile(sin, (1, 2))


def _outproj_kernel(*refs, sub, n_in):
    x_ref, mod_ref = refs[:2]
    a_refs = refs[2:2 + n_in]
    w_refs = refs[2 + n_in:2 + 2 * n_in]
    o_ref = refs[-1]
    y = _dot(a_refs[0][...], w_refs[0][...])
    for a_ref, w_ref in zip(a_refs[1:], w_refs[1:]):
        y = y + _dot(a_ref[...], w_ref[...])
    o_ref[...] = x_ref[...] + mod_ref[3 * sub + 2:3 * sub + 3, :] * y


def _outproj(x, mod_l, acts, ws, *, sub, seq_len, row0):
    n = x.shape[0]
    tm = min(512, seq_len)
    return pl.pallas_call(
        functools.partial(_outproj_kernel, sub=sub, n_in=len(acts)),
        grid=(n // tm,),
        in_specs=([pl.BlockSpec((tm, D_MODEL), lambda i: (i, 0)), _mod_spec(tm, seq_len, row0)]
                  + [pl.BlockSpec((tm, a.shape[1]), lambda i: (i, 0)) for a in acts]
                  + [_resident(w.shape) for w in ws]),
        out_specs=pl.BlockSpec((tm, D_MODEL), lambda i: (i, 0)),
        out_shape=jax.ShapeDtypeStruct((n, D_MODEL), F32),
        compiler_params=_params("parallel"),
        name="mixer_out_proj",
    )(x, mod_l, *acts, *ws)


def _attn_a_kernel(*refs, windowed):
    if windowed:
        sink_ref, q_ref, kp_ref, kc_ref, kn_ref, vp_ref, vc_ref, vn_ref, kx_ref, vx_ref, o_ref = refs
    else:
        sink_ref, q_ref, kx_ref, vx_ref, o_ref = refs
    i = pl.program_id(1)
    nb = pl.num_programs(1)
    q = (q_ref[...].astype(F32) * HEAD_DIM ** -0.5).astype(BF16)
    if windowed:
        row = lax.broadcasted_iota(jnp.int32, (BLOCK, BLOCK), 0)
        col = lax.broadcasted_iota(jnp.int32, (BLOCK, BLOCK), 1)
        mask_p = jnp.logical_and(col >= row, i > 0)
        mask_n = jnp.logical_and(col <= row, i < nb - 1)
    for g in range(A_KV_HEADS):
        gs = slice(g * HEAD_DIM, (g + 1) * HEAD_DIM)
        kx = kx_ref[:, gs].astype(BF16)
        vx = vx_ref[:, gs].astype(BF16)
        if windowed:
            kp, kc, kn = (r[:, gs].astype(BF16) for r in (kp_ref, kc_ref, kn_ref))
            vp, vc, vn = (r[:, gs].astype(BF16) for r in (vp_ref, vc_ref, vn_ref))
        for hh in range(A_GROUP):
            h = g * A_GROUP + hh
            qh = q[:, h * HEAD_DIM:(h + 1) * HEAD_DIM]
            sink = sink_ref[h]
            s_x = _dot_nt(qh, kx)
            m = jnp.maximum(jnp.max(s_x, axis=-1, keepdims=True), sink)
            if windowed:
                s_p = jnp.where(mask_p, _dot_nt(qh, kp), -jnp.inf)
                s_c = _dot_nt(qh, kc)
                s_n = jnp.where(mask_n, _dot_nt(qh, kn), -jnp.inf)
                for s in (s_p, s_c, s_n):
                    m = jnp.maximum(m, jnp.max(s, axis=-1, keepdims=True))
            e_x = jnp.exp(s_x - m)
            den = jnp.sum(e_x, axis=-1, keepdims=True) + jnp.exp(sink - m)
            acc = _dot(e_x.astype(BF16), vx)
            if windowed:
                for s, v in ((s_p, vp), (s_c, vc), (s_n, vn)):
                    e = jnp.exp(s - m)
                    den = den + jnp.sum(e, axis=-1, keepdims=True)
                    acc = acc + _dot(e.astype(BF16), v)
            o_ref[:, h * HEAD_DIM:(h + 1) * HEAD_DIM] = (acc / den).astype(o_ref.dtype)


def _attn_a(pa, sink, k_ctx, v_ctx, *, n_seq, seq_len):
    n = pa.shape[0]
    nb = seq_len // BLOCK
    kcol, vcol = A_Q // A_KV, A_Q // A_KV + 1
    smem = pl.BlockSpec(memory_space=pltpu.SMEM)
    q_spec = pl.BlockSpec((BLOCK, A_Q), lambda b, i: (b * nb + i, 0))
    if k_ctx is not None:
        n_ctx = k_ctx.shape[1]

        def blk(colblk, off):
            return pl.BlockSpec((BLOCK, A_KV), lambda b, i: (b * nb + jnp.clip(i + off, 0, nb - 1), colblk))

        in_specs = [smem, q_spec, blk(kcol, -1), blk(kcol, 0), blk(kcol, 1), blk(vcol, -1), blk(vcol, 0), blk(vcol, 1),
                    pl.BlockSpec((None, n_ctx, A_KV), lambda b, i: (b, 0, 0)),
                    pl.BlockSpec((None, n_ctx, A_KV), lambda b, i: (b, 0, 0))]
        args = [sink, pa, pa, pa, pa, pa, pa, pa, k_ctx, v_ctx]
    else:
        in_specs = [smem, q_spec,
                    pl.BlockSpec((seq_len, A_KV), lambda b, i: (b, kcol)),
                    pl.BlockSpec((seq_len, A_KV), lambda b, i: (b, vcol))]
        args = [sink, pa, pa, pa]
    return pl.pallas_call(
        functools.partial(_attn_a_kernel, windowed=k_ctx is not None),
        grid=(n_seq, nb),
        in_specs=in_specs,
        out_specs=pl.BlockSpec((BLOCK, A_Q), lambda b, i: (b * nb + i, 0)),
        out_shape=jax.ShapeDtypeStruct((n, A_Q), BF16),
        compiler_params=_params("parallel", "parallel"),
        name="mixer_a_attention",
    )(*args)


def _attn_c_kernel(*refs, lam_init, n_lat, n_ctx):
    lam_ref, subln_ref, q_ref = refs[:3]
    rest = list(refs[3:])
    kx_ref = vx_ref = kl_ref = vl_ref = None
    if n_ctx:
        kx_ref, vx_ref = rest[:2]
        rest = rest[2:]
    if n_lat:
        kl_ref, vl_ref = rest[:2]
        rest = rest[2:]
    o_ref, s1_ref, s2_ref = rest
    KEY_TILE = 512 if (n_ctx % 512 == 0 and n_lat % 512 == 0) else 256
    tiles = ([(kx_ref, vx_ref, t) for t in range(0, n_ctx, KEY_TILE)]
             + [(kl_ref, vl_ref, t) for t in range(0, n_lat, KEY_TILE)])

    lam = (jnp.exp(jnp.sum(lam_ref[0:1, :] * lam_ref[1:2, :], axis=-1, keepdims=True))
           - jnp.exp(jnp.sum(lam_ref[2:3, :] * lam_ref[3:4, :], axis=-1, keepdims=True)) + lam_init)

    q = (q_ref[...].astype(F32) * HEAD_DIM ** -0.5).astype(BF16)
    q1, q2 = q[:, :HEAD_DIM], q[:, HEAD_DIM:]
    m1 = m2 = None
    for idx, (k_ref, _, t) in enumerate(tiles):
        k = k_ref[t:t + KEY_TILE, :].astype(BF16)
        cols = slice(idx * KEY_TILE, (idx + 1) * KEY_TILE)
        s1 = _dot_nt(q1, k[:, :HEAD_DIM])
        s2 = _dot_nt(q2, k[:, HEAD_DIM:])
        s1_ref[:, cols] = s1
        s2_ref[:, cols] = s2
        r1 = jnp.max(s1, axis=-1, keepdims=True)
        r2 = jnp.max(s2, axis=-1, keepdims=True)
        m1 = r1 if m1 is None else jnp.maximum(m1, r1)
        m2 = r2 if m2 is None else jnp.maximum(m2, r2)
    l1 = l2 = 0.0
    for idx in range(len(tiles)):
        cols = slice(idx * KEY_TILE, (idx + 1) * KEY_TILE)
        e1 = jnp.exp(s1_ref[:, cols] - m1)
        e2 = jnp.exp(s2_ref[:, cols] - m2)
        s1_ref[:, cols] = e1
        s2_ref[:, cols] = e2
        l1 = l1 + jnp.sum(e1, axis=-1, keepdims=True)
        l2 = l2 + jnp.sum(e2, axis=-1, keepdims=True)
    c1 = 1.0 / l1
    c2 = lam / l2
    acc = None
    for idx, (_, v_ref, t) in enumerate(tiles):
        cols = slice(idx * KEY_TILE, (idx + 1) * KEY_TILE)
        amap = (s1_ref[:, cols] * c1 - s2_ref[:, cols] * c2).astype(BF16)
        pv = _dot(amap, v_ref[t:t + KEY_TILE, :].astype(BF16))
        acc = pv if acc is None else acc + pv
    of = acc * lax.rsqrt(jnp.mean(acc * acc, axis=-1, keepdims=True) + SUBLN_EPS) * subln_ref[...] * (1.0 - lam_init)
    o_ref[...] = of.astype(o_ref.dtype)


def _attn_c(pc, lam_vecs, subln, k_ctx, v_ctx, *, lam_init, n_seq, seq_len):
    n = pc.shape[0]
    tq = 256
    nq = seq_len // tq
    latent = k_ctx is not None
    in_specs = [pl.BlockSpec((4, HEAD_DIM), lambda b, h, i: (0, 0)),
                pl.BlockSpec((1, C_VDIM), lambda b, h, i: (0, 0)),
                pl.BlockSpec((tq, C_VDIM), lambda b, h, i: (b * nq + i, h))]
    args = [lam_vecs, subln, pc]
    own_k = pl.BlockSpec((seq_len, C_VDIM), lambda b, h, i: (b, C_HEADS + h))
    own_v = pl.BlockSpec((seq_len, C_VDIM), lambda b, h, i: (b, 2 * C_HEADS + h))
    if latent:
        n_ctx = k_ctx.shape[1]
        in_specs += [pl.BlockSpec((None, n_ctx, C_VDIM), lambda b, h, i: (b, 0, h))] * 2
        args += [k_ctx, v_ctx]
        n_lat = seq_len
    else:
        n_ctx, n_lat = seq_len, 0
    in_specs += [own_k, own_v]
    args += [pc, pc]
    return pl.pallas_call(
        functools.partial(_attn_c_kernel, lam_init=lam_init, n_lat=n_lat, n_ctx=n_ctx),
        grid=(n_seq, C_HEADS, nq),
        in_specs=in_specs,
        out_specs=pl.BlockSpec((tq, C_VDIM), lambda b, h, i: (b * nq + i, h)),
        out_shape=jax.ShapeDtypeStruct((n, C_WIDTH), BF16),
        scratch_shapes=[pltpu.VMEM((tq, n_ctx + n_lat), F32), pltpu.VMEM((tq, n_ctx + n_lat), F32)],
        compiler_params=_params("parallel", "parallel", "arbitrary"),
        name="mixer_c_attention",
    )(*args)


def _head_sum_matrix():
    idx = np.arange(B_WIDTH) // HEAD_DIM
    return jnp.asarray((idx[:, None] == idx[None, :]).astype(np.float32))


def _rwkv_prep_kernel(pb_ref, hp_ref, hn_ref, mup_ref, mun_ref, w0_ref, w2_ref, a0_ref, a2_ref,
                      kk_ref, ka_ref, rk_ref, g2_ref, bd_ref,
                      r_out, v_out, kkn_out, g_out, bonus_out, lw_out, kd_out, ab_out, *, tiles_per_seq):
    i = pl.program_id(0)
    tm = pb_ref.shape[0]
    first = i % tiles_per_seq == 0
    last = i % tiles_per_seq == tiles_per_seq - 1
    pb = pb_ref[...]
    rowi = lax.broadcasted_iota(jnp.int32, (tm, 1), 0)
    prev_row = jnp.where(first, 0.0, hp_ref[7:8, :])
    next_row = jnp.where(last, 0.0, hn_ref[0:1, :])
    prev = jnp.where(rowi == 0, prev_row, pltpu.roll(pb, 1, axis=0))
    nxt = jnp.where(rowi == tm - 1, next_row, pltpu.roll(pb, tm - 1, axis=0))
    ps = pb + mup_ref[...] * (prev - pb) + mun_ref[...] * (nxt - pb)

    o = 3 * B_WIDTH
    r = ps[:, 0:B_WIDTH]
    k = ps[:, B_WIDTH:2 * B_WIDTH]
    v = ps[:, 2 * B_WIDTH:o]
    w_dn = (ps[:, o:o + DECAY_LORA], ps[:, o + DECAY_LORA:o + 2 * DECAY_LORA])
    o += 2 * DECAY_LORA
    a_dn = (ps[:, o:o + AAA_LORA], ps[:, o + AAA_LORA:o + 2 * AAA_LORA])
    o += 2 * AAA_LORA
    gd = ps[:, o:o + GATE_LORA]

    bd = bd_ref[...]
    kk = k * kk_ref[...]
    kk = kk / jnp.maximum(jnp.sqrt(_dot(kk * kk, bd, HIGHEST)), 1e-12)
    r_out[...] = r
    v_out[...] = v
    kkn_out[...] = kk
    g_out[...] = _dot(jax.nn.sigmoid(gd), g2_ref[...], HIGHEST)
    bonus = None
    for d in range(2):
        wl = w0_ref[d:d + 1, :] + _dot(jnp.tanh(w_dn[d]), w2_ref[d], HIGHEST)
        w_log = -jax.nn.softplus(-wl) - 0.5
        lw_out[d] = -jnp.exp(w_log)
        a = jax.nn.sigmoid(a0_ref[d:d + 1, :] + _dot(a_dn[d], a2_ref[d], HIGHEST))
        k_d = k * (1.0 + (a - 1.0) * ka_ref[...])
        kd_out[d] = k_d
        ab_out[d] = a * kk
        bo = _dot(r * k_d * rk_ref[...], bd, HIGHEST) * v
        bonus = bo if bonus is None else bonus + bo
    bonus_out[...] = bonus


def _rwkv_prep(pb, W, j, *, seq_len):
    n = pb.shape[0]
    tm = 256
    tps = seq_len // tm
    nblk8 = n // 8
    row = lambda a: a.reshape(1, -1)
    res = _resident
    in_specs = [pl.BlockSpec((tm, B_COLS), lambda i: (i, 0)),
                pl.BlockSpec((8, B_COLS), lambda i: (jnp.maximum(i * (tm // 8) - 1, 0), 0)),
                pl.BlockSpec((8, B_COLS), lambda i: (jnp.minimum((i + 1) * (tm // 8), nblk8 - 1), 0)),
                res((1, B_COLS)), res((1, B_COLS)),
                res((2, B_WIDTH)), res((2, DECAY_LORA, B_WIDTH)),
                res((2, B_WIDTH)), res((2, AAA_LORA, B_WIDTH)),
                res((1, B_WIDTH)), res((1, B_WIDTH)), res((1, B_WIDTH)),
                res((GATE_LORA, B_WIDTH)), res((B_WIDTH, B_WIDTH))]
    one = pl.BlockSpec((tm, B_WIDTH), lambda i: (i, 0))
    two = pl.BlockSpec((2, tm, B_WIDTH), lambda i: (0, i, 0))
    s1 = jax.ShapeDtypeStruct((n, B_WIDTH), F32)
    s2 = jax.ShapeDtypeStruct((2, n, B_WIDTH), F32)
    return pl.pallas_call(
        functools.partial(_rwkv_prep_kernel, tiles_per_seq=tps),
        grid=(n // tm,),
        in_specs=in_specs,
        out_specs=[one, one, one, one, one, two, two, two],
        out_shape=[s1, s1, s1, s1, s1, s2, s2, s2],
        compiler_params=_params("parallel"),
        name="rwkv_prep",
    )(pb, pb, pb, row(W["b_mu_prev"][j]), row(W["b_mu_next"][j]), W["b_w0"][j], W["b_w2"][j],
      W["b_a0"][j], W["b_a2"][j], row(W["b_k_k"][j]), row(W["b_k_a"][j]), row(W["b_r_k"][j]),
      W["b_g2"][j], _head_sum_matrix())


def _bd(x):
    lane = lax.broadcasted_iota(jnp.int32, x.shape, 1)
    return jnp.concatenate([jnp.where(lane < HEAD_DIM, x, 0.0), jnp.where(lane >= HEAD_DIM, x, 0.0)], axis=0)


def _pair_diag(full):
    lane = lax.broadcasted_iota(jnp.int32, (HEAD_DIM, LANES), 1)
    return jnp.where(lane < HEAD_DIM, full[:HEAD_DIM, :], full[HEAD_DIM:, :])


def _rwkv_chunk_kernel(r_ref, v_ref, kk_ref, lw_ref, kd_ref, ab_ref, m_out, n_out, qt_out, pv_out):
    d = pl.program_id(0)
    row = lax.broadcasted_iota(jnp.int32, (CHUNK, LANES), 0)
    col = lax.broadcasted_iota(jnp.int32, (CHUNK, LANES), 1) % HEAD_DIM
    ahead = jnp.where(d == 0, row - col, col - row)
    strict = ahead > 0
    incl = ahead >= 0
    eye = (col == row).astype(F32)
    tri = incl[:, :CHUNK].astype(F32)
    hp = lambda a, b, **kw: _dot(a, b, HIGHEST)
    for p in range(PAIRS):
        ls = slice(p * LANES, (p + 1) * LANES)
        lw = lw_ref[:, ls]
        r, v, kk, kd, ab = r_ref[:, ls], v_ref[:, ls], kk_ref[:, ls], kd_ref[:, ls], ab_ref[:, ls]
        cum = hp(tri, lw)
        w_in = jnp.exp(cum)
        w_ex = jnp.exp(cum - lw)
        w_inv = jnp.exp(-cum)
        w_all = jnp.exp(jnp.sum(lw, axis=0, keepdims=True))
        alpha = w_ex * kk
        beta = ab * w_inv
        kappa = kd * w_inv
        rho = w_in * r
        bd_beta, bd_kappa = _bd(beta), _bd(kappa)
        l_ab = jnp.where(strict, _dot_nt(alpha, bd_beta, HIGHEST), 0.0)
        l_ak = jnp.where(strict, _dot_nt(alpha, bd_kappa, HIGHEST), 0.0)
        l_rk = jnp.where(incl, _dot_nt(rho, bd_kappa, HIGHEST), 0.0)
        l_rb = jnp.where(incl, _dot_nt(rho, bd_beta, HIGHEST), 0.0)
        pw = -l_ab
        t_inv = eye + pw
        for _ in range(int(math.log2(CHUNK)) - 1):
            pw = hp(pw, _bd(pw))
            t_inv = t_inv + hp(pw, _bd(t_inv))
        xa = hp(t_inv, _bd(alpha))
        xk = hp(t_inv, _bd(l_ak))
        qt_out[:, ls] = rho - hp(l_rb, _bd(xa))
        pm = l_rk - hp(l_rb, _bd(xk))
        pv_out[:, ls] = hp(pm, _bd(v))
        rs = slice(p * HEAD_DIM, (p + 1) * HEAD_DIM)
        m_out[rs, :] = (eye - _pair_diag(_dot_tn(xa, beta, HIGHEST))) * w_all
        z = kappa - _pair_diag(_dot_tn(xk, beta, HIGHEST))
        n_out[rs, :] = _pair_diag(_dot_tn(v, z, HIGHEST)) * w_all


def _rwkv_chunks(r, v, kk, lw, kd, ab, *, n_seq, seq_len):
    n = r.shape[0]
    nc = seq_len // CHUNK
    one = pl.BlockSpec((CHUNK, B_WIDTH), lambda d, b, c: (b * nc + c, 0))
    two = pl.BlockSpec((None, CHUNK, B_WIDTH), lambda d, b, c: (d, b * nc + c, 0))
    st = pl.BlockSpec((None, None, None, PAIRS * HEAD_DIM, LANES), lambda d, b, c: (d, b, c, 0, 0))
    st_shape = jax.ShapeDtypeStruct((2, n_seq, nc, PAIRS * HEAD_DIM, LANES), F32)
    tk_shape = jax.ShapeDtypeStruct((2, n, B_WIDTH), F32)
    return pl.pallas_call(
        _rwkv_chunk_kernel,
        grid=(2, n_seq, nc),
        in_specs=[one, one, one, two, two, two],
        out_specs=[st, st, two, two],
        out_shape=[st_shape, st_shape, tk_shape, tk_shape],
        compiler_params=_params("parallel", "parallel", "parallel"),
        name="rwkv_chunk_operators",
    )(r, v, kk, lw, kd, ab)


def _rwkv_scan_kernel(s0_ref, m_ref, n_ref, start_out, fin_out, s_scr):
    c = pl.program_id(2)
    SCAN_SEQS = s_scr.shape[0]

    @pl.when(c == 0)
    def _():
        s_scr[...] = s0_ref[...]

    for b in range(SCAN_SEQS):
        for p in range(PAIRS):
            rs = slice(p * HEAD_DIM, (p + 1) * HEAD_DIM)
            s = s_scr[b, rs, :]
            start_out[b, rs, :] = s
            s_scr[b, rs, :] = _dot(s, _bd(m_ref[b, rs, :]), HIGHEST) + n_ref[b, rs, :]

    @pl.when(c == pl.num_programs(2) - 1)
    def _():
        fin_out[...] = s_scr[...]


def _rwkv_scan(s0, m, nn):
    _, n_seq, nc, rows, _ = m.shape
    SCAN_SEQS = min(8, n_seq)
    chunk_of = lambda d, c: c + d * (nc - 1 - 2 * c)
    blk = pl.BlockSpec((None, SCAN_SEQS, None, rows, LANES), lambda d, g, c: (d, g, chunk_of(d, c), 0, 0))
    ends = pl.BlockSpec((None, SCAN_SEQS, rows, LANES), lambda d, g, c: (d, g, 0, 0))
    return pl.pallas_call(
        _rwkv_scan_kernel,
        grid=(2, n_seq // SCAN_SEQS, nc),
        in_specs=[ends, blk, blk],
        out_specs=[blk, ends],
        out_shape=[jax.ShapeDtypeStruct(m.shape, F32), jax.ShapeDtypeStruct((2, n_seq, rows, LANES), F32)],
        scratch_shapes=[pltpu.VMEM((SCAN_SEQS, rows, LANES), F32)],
        compiler_params=_params("parallel", "parallel", "arbitrary"),
        name="rwkv_state_scan",
    )(s0, m, nn)


def _rwkv_out_kernel(qf_ref, qb_ref, pf_ref, pb_ref, sf_ref, sb_ref, bonus_ref, g_ref, gw_ref, gb_ref, bd_ref, o_ref):
    parts = []
    for p in range(PAIRS):
        ls = slice(p * LANES, (p + 1) * LANES)
        rs = slice(p * HEAD_DIM, (p + 1) * HEAD_DIM)
        y = pf_ref[:, ls] + pb_ref[:, ls]
        y = y + _dot_nt(qf_ref[:, ls], _bd(sf_ref[rs, :]), HIGHEST)
        y = y + _dot_nt(qb_ref[:, ls], _bd(sb_ref[rs, :]), HIGHEST)
        parts.append(y)
    y = jnp.concatenate(parts, axis=1)
    mean_mat = bd_ref[...] * (1.0 / HEAD_DIM)
    mu = _dot(y, mean_mat, HIGHEST)
    yc = y - mu
    var = _dot(yc * yc, mean_mat, HIGHEST)
    yn = yc * lax.rsqrt(var + GN_EPS) * gw_ref[...] + gb_ref[...] + bonus_ref[...]
    o_ref[...] = (yn * g_ref[...]).astype(o_ref.dtype)


def _rwkv_out(qt, pv, starts, bonus, g, gn_w, gn_b, *, n_seq, seq_len):
    n = bonus.shape[0]
    nc = seq_len // CHUNK
    rows = PAIRS * HEAD_DIM
    tok = lambda d: pl.BlockSpec((None, CHUNK, B_WIDTH), lambda b, c: (d, b * nc + c, 0))
    st = lambda d: pl.BlockSpec((None, None, None, rows, LANES), lambda b, c: (d, b, c, 0, 0))
    one = pl.BlockSpec((CHUNK, B_WIDTH), lambda b, c: (b * nc + c, 0))
    vec = pl.BlockSpec((1, B_WIDTH), lambda b, c: (0, 0))
    return pl.pallas_call(
        _rwkv_out_kernel,
        grid=(n_seq, nc),
        in_specs=[tok(0), tok(1), tok(0), tok(1), st(0), st(1), one, one, vec, vec,
                  pl.BlockSpec((B_WIDTH, B_WIDTH), lambda b, c: (0, 0))],
        out_specs=one,
        out_shape=jax.ShapeDtypeStruct((n, B_WIDTH), BF16),
        compiler_params=_params("parallel", "parallel"),
        name="rwkv_output",
    )(qt, qt, pv, pv, starts, starts, bonus, g, gn_w.reshape(1, -1), gn_b.reshape(1, -1), _head_sum_matrix())


def _to_pairs(s):
    lead = s.shape[:-3]
    s = s.reshape(*lead, PAIRS, 2, HEAD_DIM, HEAD_DIM)
    s = jnp.swapaxes(s, -3, -2)
    return s.reshape(*lead, PAIRS * HEAD_DIM, LANES)


def _from_pairs(s):
    lead = s.shape[:-2]
    s = s.reshape(*lead, PAIRS, HEAD_DIM, 2, HEAD_DIM)
    s = jnp.swapaxes(s, -3, -2)
    return s.reshape(*lead, B_HEADS, HEAD_DIM, HEAD_DIM)


def _rwkv_mixer(pb, s0, j, W, *, n_seq, seq_len):
    r, v, kk, g, bonus, lw, kd, ab = _rwkv_prep(pb, W, j, seq_len=seq_len)
    m, nn, qt, pv = _rwkv_chunks(r, v, kk, lw, kd, ab, n_seq=n_seq, seq_len=seq_len)
    starts, fin = _rwkv_scan(s0, m, nn)
    ob = _rwkv_out(qt, pv, starts, bonus, g, W["b_gn_w"][j], W["b_gn_b"][j], n_seq=n_seq, seq_len=seq_len)
    return ob, fin


def _forward(x, mod, caches, W, Wb, *, row0):
    n_seq, seq_len, _ = x.shape
    latent = caches is not None
    x = x.reshape(n_seq * seq_len, D_MODEL)
    rope = _rope_tables(seq_len) if latent else None
    kw = dict(seq_len=seq_len, row0=row0)
    norm = lambda l, s: W["norms"][l, s].reshape(1, D_MODEL)
    gf = W["final_norm"].reshape(1, D_MODEL)
    ak, av, bs, ck, cv = [], [], [], [], []
    for l in range(DEPTH):
        j = l // 2
        x = _ffn(x, mod[l], norm(l, 0), Wb["ffn_w1"][l, 0], Wb["ffn_w3"][l, 0], Wb["ffn_w2"][l, 0], gf, sub=0, **kw)
        if l % 2 == 0:
            pa, pb = _proj(x, mod[l], norm(l, 1), Wb["w_in_ab"][j], rope, sub=1, splits=(A_COLS, B_COLS),
                           dtypes=(BF16 if latent else F32, F32), n_rope=(A_Q + A_KV) if latent else 0, **kw)
            if latent:
                k_ctx = caches["a_k"][:, j].reshape(n_seq, -1, A_KV)
                v_ctx = caches["a_v"][:, j].reshape(n_seq, -1, A_KV)
                s0 = _to_pairs(jnp.swapaxes(caches["b_s"][:, j], 0, 1))
            else:
                k_ctx = v_ctx = None
                s0 = jnp.zeros((2, n_seq, PAIRS * HEAD_DIM, LANES), F32)
                ak.append(pa[:, A_Q:A_Q + A_KV].reshape(n_seq, seq_len, A_KV_HEADS, HEAD_DIM))
                av.append(pa[:, A_Q + A_KV:].reshape(n_seq, seq_len, A_KV_HEADS, HEAD_DIM))
            oa = _attn_a(pa, W["a_sink"][j], k_ctx, v_ctx, n_seq=n_seq, seq_len=seq_len)
            ob, fin = _rwkv_mixer(pb, s0, j, W, n_seq=n_seq, seq_len=seq_len)
            if not latent:
                bs.append(jnp.swapaxes(_from_pairs(fin), 0, 1))
            w_out = Wb["w_out_ab"][j]
            x = _outproj(x, mod[l], [oa, ob], [w_out[:A_Q], w_out[A_Q:]], sub=1, **kw)
        else:
            (pc,) = _proj(x, mod[l], norm(l, 1), Wb["w_in_c"][j], rope, sub=1, splits=(3 * C_WIDTH,),
                          dtypes=(BF16 if latent else F32,), n_rope=2 * C_WIDTH if latent else 0, **kw)
            if latent:
                k_ctx = caches["c_k"][:, j].reshape(n_seq, -1, C_WIDTH)
                v_ctx = caches["c_v"][:, j].reshape(n_seq, -1, C_WIDTH)
            else:
                k_ctx = v_ctx = None
                ck.append(pc[:, C_WIDTH:2 * C_WIDTH].reshape(n_seq, seq_len, C_HEADS, C_VDIM))
                cv.append(pc[:, 2 * C_WIDTH:].reshape(n_seq, seq_len, C_HEADS, C_VDIM))
            lam_vecs = jnp.stack([W["c_lq1"][j], W["c_lk1"][j], W["c_lq2"][j], W["c_lk2"][j]])
            oc = _attn_c(pc, lam_vecs, W["c_subln"][j].reshape(1, C_VDIM), k_ctx, v_ctx,
                         lam_init=0.8 - 0.6 * math.exp(-0.3 * l), n_seq=n_seq, seq_len=seq_len)
            x = _outproj(x, mod[l], [oc], [Wb["w_out_c"][j]], sub=1, **kw)
        x = _ffn(x, mod[l], norm(l, 2), Wb["ffn_w1"][l, 1], Wb["ffn_w3"][l, 1], Wb["ffn_w2"][l, 1], gf,
                 sub=2, final=(l == DEPTH - 1), **kw)
    return x.reshape(n_seq, seq_len, D_MODEL), (ak, av, bs, ck, cv)


def kernel(x_prompt, x_sample, c, c_ctx, cache_a_k, cache_a_v, state_b_wkv, cache_c_k, cache_c_v, norms, mod_w, mod_b, ffn_w1, ffn_w3, ffn_w2, w_in_ab, w_out_ab, a_sink, b_mu_prev, b_mu_next, b_w0, b_w2, b_a0, b_a2, b_k_k, b_k_a, b_r_k, b_g2, b_gn_w, b_gn_b, w_in_c, w_out_c, c_lq1, c_lk1, c_lq2, c_lk2, c_subln, final_norm):
    W = dict(norms=norms, a_sink=a_sink, b_mu_prev=b_mu_prev, b_mu_next=b_mu_next,
             b_w0=b_w0, b_w2=b_w2, b_a0=b_a0, b_a2=b_a2, b_k_k=b_k_k, b_k_a=b_k_a, b_r_k=b_r_k,
             b_g2=b_g2, b_gn_w=b_gn_w, b_gn_b=b_gn_b,
             c_lq1=c_lq1, c_lk1=c_lk1, c_lq2=c_lq2, c_lk2=c_lk2, c_subln=c_subln, final_norm=final_norm)
    Wb = dict(ffn_w1=ffn_w1.astype(BF16), ffn_w3=ffn_w3.astype(BF16), ffn_w2=ffn_w2.astype(BF16),
              w_in_ab=w_in_ab.astype(BF16), w_out_ab=w_out_ab.astype(BF16),
              w_in_c=w_in_c.astype(BF16), w_out_c=w_out_c.astype(BF16))
    n_lat = c.shape[0]
    assert 1 + n_lat <= MOD_ROWS
    cond = jnp.zeros((MOD_ROWS, D_MODEL), F32).at[0].set(c_ctx).at[1:1 + n_lat].set(c)
    mod = _modulation(cond, mod_w, mod_b)
    y_prompt, (ak, av, bs, ck, cv) = _forward(x_prompt, mod, None, W, Wb, row0=0)
    caches = dict(a_k=cache_a_k, a_v=cache_a_v, b_s=state_b_wkv, c_k=cache_c_k, c_v=cache_c_v)
    y_sample, _ = _forward(x_sample, mod, caches, W, Wb, row0=1)
    return (y_prompt, y_sample, jnp.stack(ak, axis=1), jnp.stack(av, axis=1), jnp.stack(bs, axis=1),
            jnp.stack(ck, axis=1), jnp.stack(cv, axis=1))
```

```python
import functools
import math

import numpy as np
import jax
import jax.numpy as jnp
from jax import lax
from jax.experimental import pallas as pl
from jax.experimental.pallas import tpu as pltpu

F32 = jnp.float32
BF16 = jnp.bfloat16
HIGHEST = lax.Precision.HIGHEST

D_MODEL = 1024
DEPTH = 4
GRID_W = 64
HEAD_DIM = 64
BLOCK = 128
A_HEADS = 8
A_KV_HEADS = 2
A_GROUP = A_HEADS // A_KV_HEADS
A_Q = A_HEADS * HEAD_DIM
A_KV = A_KV_HEADS * HEAD_DIM
A_COLS = A_Q + 2 * A_KV
B_HEADS = 8
B_WIDTH = B_HEADS * HEAD_DIM
DECAY_LORA = 64
AAA_LORA = 64
GATE_LORA = 128
B_COLS = 3 * B_WIDTH + 2 * DECAY_LORA + 2 * AAA_LORA + GATE_LORA
C_HEADS = 8
C_VDIM = 2 * HEAD_DIM
C_WIDTH = C_HEADS * C_VDIM
D_FF = 2816
N_MOD = 9
ROPE_THETA = 10000.0
NORM_EPS = 1e-6
GN_EPS = 64e-5
SUBLN_EPS = 1e-5

LANES = 128
MOD_ROWS = 16
CHUNK = 64
PAIRS = B_HEADS // 2
VMEM_LIMIT = 56 * 2 ** 20


def _params(*sem):
    return pltpu.CompilerParams(dimension_semantics=sem, vmem_limit_bytes=VMEM_LIMIT)


def _dot(a, b, precision=None):
    return jnp.dot(a, b, preferred_element_type=F32, precision=precision)


def _dot_nt(a, b, precision=None):
    return lax.dot_general(a, b, (((1,), (1,)), ((), ())), preferred_element_type=F32, precision=precision)


def _dot_tn(a, b, precision=None):
    return lax.dot_general(a, b, (((0,), (0,)), ((), ())), preferred_element_type=F32, precision=precision)


def _silu(x):
    return x * jax.nn.sigmoid(x)


def _modnorm(x, g, scale, shift):
    y = x * lax.rsqrt(jnp.mean(x * x, axis=-1, keepdims=True) + NORM_EPS)
    return (y * g) * (1.0 + scale) + shift


def _mod_kernel(c_ref, w_ref, b_ref, o_ref):
    o_ref[...] = _dot(_silu(c_ref[...]), w_ref[...], HIGHEST) + b_ref[...]


def _modulation(cond, mod_w, mod_b):
    n_col = N_MOD * D_MODEL
    tn = D_MODEL
    out = pl.pallas_call(
        _mod_kernel,
        grid=(DEPTH, n_col // tn),
        in_specs=[pl.BlockSpec((MOD_ROWS, D_MODEL), lambda l, j: (0, 0)),
                  pl.BlockSpec((None, D_MODEL, tn), lambda l, j: (l, 0, j)),
                  pl.BlockSpec((None, 1, tn), lambda l, j: (l, 0, j))],
        out_specs=pl.BlockSpec((None, MOD_ROWS, tn), lambda l, j: (l, 0, j)),
        out_shape=jax.ShapeDtypeStruct((DEPTH, MOD_ROWS, n_col), F32),
        compiler_params=_params("parallel", "parallel"),
        name="modulation",
    )(cond, mod_w, mod_b.reshape(DEPTH, 1, n_col))
    return out.reshape(DEPTH, MOD_ROWS, N_MOD, D_MODEL)


def _mod_spec(tm, seq_len, row0):
    if row0 == 0:
        return pl.BlockSpec((None, N_MOD, D_MODEL), lambda i: (0, 0, 0))
    return pl.BlockSpec((None, N_MOD, D_MODEL), lambda i: (row0 + (i * tm) // seq_len, 0, 0))


def _resident(shape):
    nd = len(shape)
    return pl.BlockSpec(shape, lambda i: (0,) * nd)


FF_TILE = 256


def _ffn_kernel(x_ref, mod_ref, g_ref, w1_ref, w3_ref, w2_ref, gf_ref, o_ref, act_ref, *, sub, final):
    x = x_ref[...]
    h = _modnorm(x, g_ref[...], mod_ref[3 * sub + 1:3 * sub + 2, :], mod_ref[3 * sub:3 * sub + 1, :]).astype(BF16)
    for j in range(D_FF // FF_TILE):
        cols = slice(j * FF_TILE, (j + 1) * FF_TILE)
        a = _dot(h, w1_ref[:, cols])
        b = _dot(h, w3_ref[:, cols])
        act_ref[:, cols] = (_silu(a) * b).astype(BF16)
    y = x + 0.5 * mod_ref[3 * sub + 2:3 * sub + 3, :] * _dot(act_ref[...], w2_ref[...])
    if final:
        y = y * lax.rsqrt(jnp.mean(y * y, axis=-1, keepdims=True) + NORM_EPS) * gf_ref[...]
    o_ref[...] = y


def _ffn(x, mod_l, g, w1, w3, w2, gf, *, sub, seq_len, row0, final=False):
    n = x.shape[0]
    tm = min(512, seq_len)
    return pl.pallas_call(
        functools.partial(_ffn_kernel, sub=sub, final=final),
        grid=(n // tm,),
        in_specs=[pl.BlockSpec((tm, D_MODEL), lambda i: (i, 0)),
                  _mod_spec(tm, seq_len, row0),
                  _resident((1, D_MODEL)),
                  _resident((D_MODEL, D_FF)),
                  _resident((D_MODEL, D_FF)),
                  _resident((D_FF, D_MODEL)),
                  _resident((1, D_MODEL))],
        out_specs=pl.BlockSpec((tm, D_MODEL), lambda i: (i, 0)),
        out_shape=jax.ShapeDtypeStruct((n, D_MODEL), F32),
        scratch_shapes=[pltpu.VMEM((tm, D_FF), BF16)],
        compiler_params=_params("parallel"),
        name="ffn",
    )(x, mod_l, g, w1, w3, w2, gf)


def _swap16(x):
    lane = lax.broadcasted_iota(jnp.int32, x.shape, 1)
    return jnp.where(lane % 32 < 16, pltpu.roll(x, LANES - 16, axis=1), pltpu.roll(x, 16, axis=1))


def _proj_kernel(*refs, sub, splits, n_rope):
    if n_rope:
        x_ref, mod_ref, g_ref, w_ref, cos_ref, sin_ref = refs[:6]
        o_refs = refs[6:]
    else:
        x_ref, mod_ref, g_ref, w_ref = refs[:4]
        o_refs = refs[4:]
    h = _modnorm(x_ref[...], g_ref[...], mod_ref[3 * sub + 1:3 * sub + 2, :], mod_ref[3 * sub:3 * sub + 1, :]).astype(BF16)
    start = 0
    for o_ref, width in zip(o_refs, splits):
        for c in range(0, width, 256):
            wd = min(256, width - c)
            y = _dot(h, w_ref[:, start + c:start + c + wd])
            for p in range(0, wd, LANES):
                yp = y[:, p:p + LANES]
                if start + c + p < n_rope:
                    yp = yp * cos_ref[...] + _swap16(yp) * sin_ref[...]
                o_ref[:, c + p:c + p + LANES] = yp.astype(o_ref.dtype)
        start += width


def _proj(x, mod_l, g, w, rope, *, sub, splits, dtypes, n_rope, seq_len, row0):
    n = x.shape[0]
    tm = min(512, seq_len)
    n_col = w.shape[1]
    in_specs = [pl.BlockSpec((tm, D_MODEL), lambda i: (i, 0)),
                _mod_spec(tm, seq_len, row0),
                _resident((1, D_MODEL)),
                _resident((D_MODEL, n_col))]
    args = [x, mod_l, g, w]
    if n_rope:
        tiles = seq_len // tm
        in_specs += [pl.BlockSpec((tm, LANES), lambda i: (i % tiles, 0))] * 2
        args += list(rope)
    return pl.pallas_call(
        functools.partial(_proj_kernel, sub=sub, splits=splits, n_rope=n_rope),
        grid=(n // tm,),
        in_specs=in_specs,
        out_specs=[pl.BlockSpec((tm, wd), lambda i: (i, 0)) for wd in splits],
        out_shape=[jax.ShapeDtypeStruct((n, wd), dt) for wd, dt in zip(splits, dtypes)],
        compiler_params=_params("parallel"),
        name="mixer_in_proj",
    )(*args)


def _rope_tables(n_tok):
    t = jnp.arange(n_tok, dtype=jnp.int32)
    row = (t // GRID_W).astype(F32)
    col = (t % GRID_W).astype(F32)
    n_freq = HEAD_DIM // 4
    inv = jnp.power(ROPE_THETA, -jnp.arange(n_freq, dtype=F32) / n_freq)
    ar, ac = row[:, None] * inv, col[:, None] * inv
    cos = jnp.concatenate([jnp.cos(ar), jnp.cos(ar), jnp.cos(ac), jnp.cos(ac)], axis=1)
    sin = jnp.concatenate([-jnp.sin(ar), jnp.sin(ar), -jnp.sin(ac), jnp.sin(ac)], axis=1)
    return jnp.tile(cos, (1, 2)), jnp.tile(sin, (1, 2))


def _outproj_kernel(*refs, sub, n_in):
    x_ref, mod_ref = refs[:2]
    a_refs = refs[2:2 + n_in]
    w_refs = refs[2 + n_in:2 + 2 * n_in]
    o_ref = refs[-1]
    y = _dot(a_refs[0][...], w_refs[0][...])
    for a_ref, w_ref in zip(a_refs[1:], w_refs[1:]):
        y = y + _dot(a_ref[...], w_ref[...])
    o_ref[...] = x_ref[...] + mod_ref[3 * sub + 2:3 * sub + 3, :] * y


def _outproj(x, mod_l, acts, ws, *, sub, seq_len, row0):
    n = x.shape[0]
    tm = min(512, seq_len)
    return pl.pallas_call(
        functools.partial(_outproj_kernel, sub=sub, n_in=len(acts)),
        grid=(n // tm,),
        in_specs=([pl.BlockSpec((tm, D_MODEL), lambda i: (i, 0)), _mod_spec(tm, seq_len, row0)]
                  + [pl.BlockSpec((tm, a.shape[1]), lambda i: (i, 0)) for a in acts]
                  + [_resident(w.shape) for w in ws]),
        out_specs=pl.BlockSpec((tm, D_MODEL), lambda i: (i, 0)),
        out_shape=jax.ShapeDtypeStruct((n, D_MODEL), F32),
        compiler_params=_params("parallel"),
        name="mixer_out_proj",
    )(x, mod_l, *acts, *ws)


def _attn_a_kernel(*refs, windowed):
    if windowed:
        sink_ref, q_ref, kp_ref, kc_ref, kn_ref, vp_ref, vc_ref, vn_ref, kx_ref, vx_ref, o_ref = refs
    else:
        sink_ref, q_ref, kx_ref, vx_ref, o_ref = refs
    i = pl.program_id(1)
    nb = pl.num_programs(1)
    q = (q_ref[...].astype(F32) * HEAD_DIM ** -0.5).astype(BF16)
    if windowed:
        row = lax.broadcasted_iota(jnp.int32, (BLOCK, BLOCK), 0)
        col = lax.broadcasted_iota(jnp.int32, (BLOCK, BLOCK), 1)
        mask_p = jnp.logical_and(col >= row, i > 0)
        mask_n = jnp.logical_and(col <= row, i < nb - 1)
    for g in range(A_KV_HEADS):
        gs = slice(g * HEAD_DIM, (g + 1) * HEAD_DIM)
        kx = kx_ref[:, gs].astype(BF16)
        vx = vx_ref[:, gs].astype(BF16)
        if windowed:
            kp, kc, kn = (r[:, gs].astype(BF16) for r in (kp_ref, kc_ref, kn_ref))
            vp, vc, vn = (r[:, gs].astype(BF16) for r in (vp_ref, vc_ref, vn_ref))
        for hh in range(A_GROUP):
            h = g * A_GROUP + hh
            qh = q[:, h * HEAD_DIM:(h + 1) * HEAD_DIM]
            sink = sink_ref[h]
            s_x = _dot_nt(qh, kx)
            m = jnp.maximum(jnp.max(s_x, axis=-1, keepdims=True), sink)
            if windowed:
                s_p = jnp.where(mask_p, _dot_nt(qh, kp), -jnp.inf)
                s_c = _dot_nt(qh, kc)
                s_n = jnp.where(mask_n, _dot_nt(qh, kn), -jnp.inf)
                for s in (s_p, s_c, s_n):
                    m = jnp.maximum(m, jnp.max(s, axis=-1, keepdims=True))
            e_x = jnp.exp(s_x - m)
            den = jnp.sum(e_x, axis=-1, keepdims=True) + jnp.exp(sink - m)
            acc = _dot(e_x.astype(BF16), vx)
            if windowed:
                for s, v in ((s_p, vp), (s_c, vc), (s_n, vn)):
                    e = jnp.exp(s - m)
                    den = den + jnp.sum(e, axis=-1, keepdims=True)
                    acc = acc + _dot(e.astype(BF16), v)
            o_ref[:, h * HEAD_DIM:(h + 1) * HEAD_DIM] = (acc / den).astype(o_ref.dtype)


def _attn_a(pa, sink, k_ctx, v_ctx, *, n_seq, seq_len):
    n = pa.shape[0]
    nb = seq_len // BLOCK
    kcol, vcol = A_Q // A_KV, A_Q // A_KV + 1
    smem = pl.BlockSpec(memory_space=pltpu.SMEM)
    q_spec = pl.BlockSpec((BLOCK, A_Q), lambda b, i: (b * nb + i, 0))
    if k_ctx is not None:
        n_ctx = k_ctx.shape[1]

        def blk(colblk, off):
            return pl.BlockSpec((BLOCK, A_KV), lambda b, i: (b * nb + jnp.clip(i + off, 0, nb - 1), colblk))

        in_specs = [smem, q_spec, blk(kcol, -1), blk(kcol, 0), blk(kcol, 1), blk(vcol, -1), blk(vcol, 0), blk(vcol, 1),
                    pl.BlockSpec((None, n_ctx, A_KV), lambda b, i: (b, 0, 0)),
                    pl.BlockSpec((None, n_ctx, A_KV), lambda b, i: (b, 0, 0))]
        args = [sink, pa, pa, pa, pa, pa, pa, pa, k_ctx, v_ctx]
    else:
        in_specs = [smem, q_spec,
                    pl.BlockSpec((seq_len, A_KV), lambda b, i: (b, kcol)),
                    pl.BlockSpec((seq_len, A_KV), lambda b, i: (b, vcol))]
        args = [sink, pa, pa, pa]
    return pl.pallas_call(
        functools.partial(_attn_a_kernel, windowed=k_ctx is not None),
        grid=(n_seq, nb),
        in_specs=in_specs,
        out_specs=pl.BlockSpec((BLOCK, A_Q), lambda b, i: (b * nb + i, 0)),
        out_shape=jax.ShapeDtypeStruct((n, A_Q), BF16),
        compiler_params=_params("parallel", "parallel"),
        name="mixer_a_attention",
    )(*args)


def _attn_c_kernel(*refs, lam_init, n_lat, n_ctx):
    lam_ref, subln_ref, q_ref = refs[:3]
    rest = list(refs[3:])
    kx_ref = vx_ref = kl_ref = vl_ref = None
    if n_ctx:
        kx_ref, vx_ref = rest[:2]
        rest = rest[2:]
    if n_lat:
        kl_ref, vl_ref = rest[:2]
        rest = rest[2:]
    o_ref, s_ref, e_ref = rest
    KEY_TILE = 512 if (n_ctx % 512 == 0 and n_lat % 512 == 0) else 256
    tiles = ([(kx_ref, vx_ref, t) for t in range(0, n_ctx, KEY_TILE)]
             + [(kl_ref, vl_ref, t) for t in range(0, n_lat, KEY_TILE)])
    th = q_ref.shape[0] // 2
    n_tiles = len(tiles)
    cols = [slice(i * KEY_TILE, (i + 1) * KEY_TILE) for i in range(n_tiles)]

    lam = (jnp.exp(jnp.sum(lam_ref[0:1, :] * lam_ref[1:2, :], axis=-1, keepdims=True))
           - jnp.exp(jnp.sum(lam_ref[2:3, :] * lam_ref[3:4, :], axis=-1, keepdims=True)) + lam_init)

    qz, m, l, ratio, acc = [None] * 2, [None] * 2, [0.0] * 2, [None] * 2, [None] * 2
    for hf in range(2):
        q = (q_ref[hf * th:(hf + 1) * th, :].astype(F32) * HEAD_DIM ** -0.5).astype(BF16)
        lane = lax.broadcasted_iota(jnp.int32, q.shape, 1)
        zero = jnp.zeros_like(q)
        qz[hf] = jnp.concatenate([jnp.where(lane < HEAD_DIM, q, zero), jnp.where(lane >= HEAD_DIM, q, zero)], axis=0)

    def scores(hf, i):
        k_ref, _, t = tiles[i]
        s = _dot_nt(qz[hf], k_ref[t:t + KEY_TILE, :].astype(BF16))
        s_ref[hf, :, cols[i]] = s
        r = jnp.max(s, axis=-1, keepdims=True)
        m[hf] = r if m[hf] is None else jnp.maximum(m[hf], r)

    def exps(hf, i):
        e = jnp.exp(s_ref[hf, :, cols[i]] - m[hf])
        l[hf] = l[hf] + jnp.sum(e, axis=-1, keepdims=True)
        e_ref[hf, :, cols[i]] = e.astype(BF16)

    def set_ratio(hf):
        ratio[hf] = jnp.broadcast_to(lam * l[hf][:th] / l[hf][th:], (th, KEY_TILE)).astype(BF16)

    def values(hf, i):
        _, v_ref, t = tiles[i]
        amap = e_ref[hf, :th, cols[i]] - e_ref[hf, th:, cols[i]] * ratio[hf]
        pv = _dot(amap, v_ref[t:t + KEY_TILE, :].astype(BF16))
        acc[hf] = pv if acc[hf] is None else acc[hf] + pv

    for i in range(n_tiles):
        scores(0, i)
    for i in range(n_tiles):
        scores(1, i)
        exps(0, i)
    set_ratio(0)
    for i in range(n_tiles):
        exps(1, i)
        values(0, i)
    set_ratio(1)
    for i in range(n_tiles):
        values(1, i)
    for hf in range(2):
        o = acc[hf] / l[hf][:th]
        of = o * lax.rsqrt(jnp.mean(o * o, axis=-1, keepdims=True) + SUBLN_EPS) * subln_ref[...] * (1.0 - lam_init)
        o_ref[hf * th:(hf + 1) * th, :] = of.astype(o_ref.dtype)


def _attn_c(pc, lam_vecs, subln, k_ctx, v_ctx, *, lam_init, n_seq, seq_len):
    n = pc.shape[0]
    tq = 256
    nq = seq_len // tq
    latent = k_ctx is not None
    in_specs = [pl.BlockSpec((4, HEAD_DIM), lambda b, h, i: (0, 0)),
                pl.BlockSpec((1, C_VDIM), lambda b, h, i: (0, 0)),
                pl.BlockSpec((tq, C_VDIM), lambda b, h, i: (b * nq + i, h))]
    args = [lam_vecs, subln, pc]
    own_k = pl.BlockSpec((seq_len, C_VDIM), lambda b, h, i: (b, C_HEADS + h))
    own_v = pl.BlockSpec((seq_len, C_VDIM), lambda b, h, i: (b, 2 * C_HEADS + h))
    if latent:
        n_ctx = k_ctx.shape[1]
        in_specs += [pl.BlockSpec((None, n_ctx, C_VDIM), lambda b, h, i: (b, 0, h))] * 2
        args += [k_ctx, v_ctx]
        n_lat = seq_len
    else:
        n_ctx, n_lat = seq_len, 0
    in_specs += [own_k, own_v]
    args += [pc, pc]
    return pl.pallas_call(
        functools.partial(_attn_c_kernel, lam_init=lam_init, n_lat=n_lat, n_ctx=n_ctx),
        grid=(n_seq, C_HEADS, nq),
        in_specs=in_specs,
        out_specs=pl.BlockSpec((tq, C_VDIM), lambda b, h, i: (b * nq + i, h)),
        out_shape=jax.ShapeDtypeStruct((n, C_WIDTH), BF16),
        scratch_shapes=[pltpu.VMEM((2, tq, n_ctx + n_lat), F32), pltpu.VMEM((2, tq, n_ctx + n_lat), BF16)],
        compiler_params=_params("parallel", "parallel", "arbitrary"),
        name="mixer_c_attention",
    )(*args)


def _pair_ones():
    idx = np.arange(LANES) // HEAD_DIM
    ones = (idx[:, None] == idx[None, :]).astype(np.float32)
    return jnp.asarray(np.concatenate([ones, ones], axis=0), dtype=BF16)


def _head_sum(x, ones2):
    out = []
    for p in range(PAIRS):
        hi, lo = _split(x[:, p * LANES:(p + 1) * LANES])
        out.append(_dot(jnp.concatenate([hi, lo], axis=1), ones2))
    return jnp.concatenate(out, axis=1)


def _rwkv_prep_kernel(pb_ref, hp_ref, hn_ref, mup_ref, mun_ref, w0_ref, w2_ref, a0_ref, a2_ref,
                      kk_ref, ka_ref, rk_ref, g2_ref, bd_ref,
                      r_out, v_out, kkn_out, g_out, bonus_out, lw_out, kd_out, ab_out, *, tiles_per_seq):
    i = pl.program_id(0)
    tm = pb_ref.shape[0]
    first = i % tiles_per_seq == 0
    last = i % tiles_per_seq == tiles_per_seq - 1
    pb = pb_ref[...]
    rowi = lax.broadcasted_iota(jnp.int32, (tm, 1), 0)
    prev_row = jnp.where(first, 0.0, hp_ref[7:8, :])
    next_row = jnp.where(last, 0.0, hn_ref[0:1, :])
    prev = jnp.where(rowi == 0, prev_row, pltpu.roll(pb, 1, axis=0))
    nxt = jnp.where(rowi == tm - 1, next_row, pltpu.roll(pb, tm - 1, axis=0))
    ps = pb + mup_ref[...] * (prev - pb) + mun_ref[...] * (nxt - pb)

    o = 3 * B_WIDTH
    r = ps[:, 0:B_WIDTH]
    k = ps[:, B_WIDTH:2 * B_WIDTH]
    v = ps[:, 2 * B_WIDTH:o]
    w_dn = (ps[:, o:o + DECAY_LORA], ps[:, o + DECAY_LORA:o + 2 * DECAY_LORA])
    o += 2 * DECAY_LORA
    a_dn = (ps[:, o:o + AAA_LORA], ps[:, o + AAA_LORA:o + 2 * AAA_LORA])
    o += 2 * AAA_LORA
    gd = ps[:, o:o + GATE_LORA]

    ones2 = bd_ref[...]
    kk = k * kk_ref[...]
    kk = kk / jnp.maximum(jnp.sqrt(_head_sum(kk * kk, ones2)), 1e-12)
    r_out[...] = r
    v_out[...] = v
    kkn_out[...] = kk
    g_out[...] = _mm3(jax.nn.sigmoid(gd), g2_ref[...])
    bonus = None
    for d in range(2):
        wl = w0_ref[d:d + 1, :] + _mm3(jnp.tanh(w_dn[d]), w2_ref[d])
        w_log = -jax.nn.softplus(-wl) - 0.5
        lw_out[d] = -jnp.exp(w_log)
        a = jax.nn.sigmoid(a0_ref[d:d + 1, :] + _mm3(a_dn[d], a2_ref[d]))
        k_d = k * (1.0 + (a - 1.0) * ka_ref[...])
        kd_out[d] = k_d
        ab_out[d] = a * kk
        bo = _head_sum(r * k_d * rk_ref[...], ones2) * v
        bonus = bo if bonus is None else bonus + bo
    bonus_out[...] = bonus


def _rwkv_prep(pb, W, j, *, seq_len):
    n = pb.shape[0]
    tm = 256
    tps = seq_len // tm
    nblk8 = n // 8
    row = lambda a: a.reshape(1, -1)
    res = _resident
    in_specs = [pl.BlockSpec((tm, B_COLS), lambda i: (i, 0)),
                pl.BlockSpec((8, B_COLS), lambda i: (jnp.maximum(i * (tm // 8) - 1, 0), 0)),
                pl.BlockSpec((8, B_COLS), lambda i: (jnp.minimum((i + 1) * (tm // 8), nblk8 - 1), 0)),
                res((1, B_COLS)), res((1, B_COLS)),
                res((2, B_WIDTH)), res((2, DECAY_LORA, B_WIDTH)),
                res((2, B_WIDTH)), res((2, AAA_LORA, B_WIDTH)),
                res((1, B_WIDTH)), res((1, B_WIDTH)), res((1, B_WIDTH)),
                res((GATE_LORA, B_WIDTH)), res((2 * LANES, LANES))]
    one = pl.BlockSpec((tm, B_WIDTH), lambda i: (i, 0))
    two = pl.BlockSpec((2, tm, B_WIDTH), lambda i: (0, i, 0))
    s1 = jax.ShapeDtypeStruct((n, B_WIDTH), F32)
    s2 = jax.ShapeDtypeStruct((2, n, B_WIDTH), F32)
    return pl.pallas_call(
        functools.partial(_rwkv_prep_kernel, tiles_per_seq=tps),
        grid=(n // tm,),
        in_specs=in_specs,
        out_specs=[one, one, one, one, one, two, two, two],
        out_shape=[s1, s1, s1, s1, s1, s2, s2, s2],
        compiler_params=_params("parallel"),
        name="rwkv_prep",
    )(pb, pb, pb, row(W["b_mu_prev"][j]), row(W["b_mu_next"][j]), W["b_w0"][j], W["b_w2"][j],
      W["b_a0"][j], W["b_a2"][j], row(W["b_k_k"][j]), row(W["b_k_a"][j]), row(W["b_r_k"][j]),
      W["b_g2"][j], _pair_ones())


def _bd(x):
    lane = lax.broadcasted_iota(jnp.int32, x.shape, 1)
    return jnp.concatenate([jnp.where(lane < HEAD_DIM, x, 0.0), jnp.where(lane >= HEAD_DIM, x, 0.0)], axis=0)


def _pair_diag(full):
    lane = lax.broadcasted_iota(jnp.int32, (HEAD_DIM, LANES), 1)
    return jnp.where(lane < HEAD_DIM, full[:HEAD_DIM, :], full[HEAD_DIM:, :])


def _split(x):
    hi = x.astype(BF16)
    return hi, (x - hi.astype(F32)).astype(BF16)


def _mm3(a, b, bd=False):
    (ah, al), (bh, bl) = _split(a), _split(b)
    if bd:
        bh, bl = _bd(bh), _bd(bl)
    return _dot(jnp.concatenate([ah, al, ah], axis=1), jnp.concatenate([bh, bh, bl], axis=0))


def _mm3_nt(a, b):
    (ah, al), (bh, bl) = _split(a), _split(b)
    bh, bl = _bd(bh), _bd(bl)
    return _dot_nt(jnp.concatenate([ah, al, ah], axis=1), jnp.concatenate([bh, bh, bl], axis=1))


def _mm3_tn(a, b):
    (ah, al), (bh, bl) = _split(a), _split(b)
    return _dot_tn(jnp.concatenate([ah, al, ah], axis=0), jnp.concatenate([bh, bh, bl], axis=0))


def _rwkv_chunk_kernel(r_ref, v_ref, kk_ref, lw_ref, kd_ref, ab_ref, m_out, n_out, qt_out, pv_out):
    d = pl.program_id(0)
    row = lax.broadcasted_iota(jnp.int32, (CHUNK, LANES), 0)
    col = lax.broadcasted_iota(jnp.int32, (CHUNK, LANES), 1) % HEAD_DIM
    ahead = jnp.where(d == 0, row - col, col - row)
    strict = ahead > 0
    incl = ahead >= 0
    eye = (col == row).astype(F32)
    tri = jnp.where(incl[:, :CHUNK], 1.0, 0.0).astype(BF16)
    tri3 = jnp.concatenate([tri, tri, tri], axis=1)
    pairs = range(PAIRS)
    lanes = [slice(p * LANES, (p + 1) * LANES) for p in pairs]
    rows = [slice(p * HEAD_DIM, (p + 1) * HEAD_DIM) for p in pairs]
    H = CHUNK

    lw = [lw_ref[:, ls] for ls in lanes]
    cum = []
    for x in lw:
        hi = x.astype(BF16)
        r1 = x - hi.astype(F32)
        mid = r1.astype(BF16)
        lo = (r1 - mid.astype(F32)).astype(BF16)
        cum.append(_dot(tri3, jnp.concatenate([hi, mid, lo], axis=0)))
    w_inv = [jnp.exp(-c) for c in cum]
    w_all = [jnp.exp(jnp.sum(x, axis=0, keepdims=True)) for x in lw]
    alpha = [jnp.exp(c - x) * kk_ref[:, ls] for c, x, ls in zip(cum, lw, lanes)]
    beta = [ab_ref[:, ls] * w for w, ls in zip(w_inv, lanes)]
    kappa = [kd_ref[:, ls] * w for w, ls in zip(w_inv, lanes)]
    rho = [jnp.exp(c) * r_ref[:, ls] for c, ls in zip(cum, lanes)]

    ar = [jnp.concatenate([a, r], axis=0) for a, r in zip(alpha, rho)]
    by_b = [_mm3_nt(x, b) for x, b in zip(ar, beta)]
    by_k = [_mm3_nt(x, k) for x, k in zip(ar, kappa)]
    l_ab = [jnp.where(strict, y[:H], 0.0) for y in by_b]
    l_rb = [jnp.where(incl, y[H:], 0.0) for y in by_b]
    l_ak = [jnp.where(strict, y[:H], 0.0) for y in by_k]
    l_rk = [jnp.where(incl, y[H:], 0.0) for y in by_k]

    pw = [-x for x in l_ab]
    t_inv = [eye + x for x in pw]
    pw = [_mm3(x, x, bd=True) for x in pw]
    for _ in range(int(math.log2(CHUNK)) - 2):
        both = [_mm3(jnp.concatenate([t, x], axis=0), x, bd=True) for t, x in zip(t_inv, pw)]
        t_inv = [t + y[:H] for t, y in zip(t_inv, both)]
        pw = [y[H:] for y in both]
    t_inv = [t + _mm3(t, x, bd=True) for t, x in zip(t_inv, pw)]

    xa = [_mm3(t, a, bd=True) for t, a in zip(t_inv, alpha)]
    xk = [_mm3(t, x, bd=True) for t, x in zip(t_inv, l_ak)]
    for p in pairs:
        qt_out[:, lanes[p]] = (rho[p] - _mm3(l_rb[p], xa[p], bd=True)).astype(qt_out.dtype)
    pm = [lk - _mm3(lb, x, bd=True) for lk, lb, x in zip(l_rk, l_rb, xk)]
    for p in pairs:
        pv_out[:, lanes[p]] = _dot(pm[p].astype(BF16), _bd(v_ref[:, lanes[p]].astype(BF16)))
    g = [_mm3_tn(jnp.concatenate([a, k], axis=1), b) for a, k, b in zip(xa, xk, beta)]
    for p in pairs:
        m_out[rows[p], :] = (eye - _pair_diag(g[p][:LANES])) * w_all[p]
    z = [k - _pair_diag(y[LANES:]) for k, y in zip(kappa, g)]
    for p in pairs:
        n_out[rows[p], :] = _pair_diag(_mm3_tn(v_ref[:, lanes[p]], z[p])) * w_all[p]


def _rwkv_chunks(r, v, kk, lw, kd, ab, *, n_seq, seq_len):
    n = r.shape[0]
    nc = seq_len // CHUNK
    one = pl.BlockSpec((CHUNK, B_WIDTH), lambda d, b, c: (b * nc + c, 0))
    two = pl.BlockSpec((None, CHUNK, B_WIDTH), lambda d, b, c: (d, b * nc + c, 0))
    st = pl.BlockSpec((None, None, None, PAIRS * HEAD_DIM, LANES), lambda d, b, c: (d, b, c, 0, 0))
    st_shape = jax.ShapeDtypeStruct((2, n_seq, nc, PAIRS * HEAD_DIM, LANES), F32)
    return pl.pallas_call(
        _rwkv_chunk_kernel,
        grid=(2, n_seq, nc),
        in_specs=[one, one, one, two, two, two],
        out_specs=[st, st, two, two],
        out_shape=[st_shape, st_shape, jax.ShapeDtypeStruct((2, n, B_WIDTH), BF16),
                   jax.ShapeDtypeStruct((2, n, B_WIDTH), F32)],
        compiler_params=_params("parallel", "parallel", "parallel"),
        name="rwkv_chunk_operators",
    )(r, v, kk, lw, kd, ab)


def _rwkv_scan_kernel(s0_ref, m_ref, n_ref, start_out, fin_out, s_scr):
    c = pl.program_id(2)
    SCAN_SEQS = s_scr.shape[0]

    @pl.when(c == 0)
    def _():
        s_scr[...] = s0_ref[...]

    for b in range(SCAN_SEQS):
        for p in range(PAIRS):
            rs = slice(p * HEAD_DIM, (p + 1) * HEAD_DIM)
            s = s_scr[b, rs, :]
            start_out[b, rs, :] = s
            s_scr[b, rs, :] = _mm3(s, m_ref[b, rs, :], bd=True) + n_ref[b, rs, :]

    @pl.when(c == pl.num_programs(2) - 1)
    def _():
        fin_out[...] = s_scr[...]


def _rwkv_scan(s0, m, nn):
    _, n_seq, nc, rows, _ = m.shape
    SCAN_SEQS = min(8, n_seq)
    chunk_of = lambda d, c: c + d * (nc - 1 - 2 * c)
    blk = pl.BlockSpec((None, SCAN_SEQS, None, rows, LANES), lambda d, g, c: (d, g, chunk_of(d, c), 0, 0))
    ends = pl.BlockSpec((None, SCAN_SEQS, rows, LANES), lambda d, g, c: (d, g, 0, 0))
    return pl.pallas_call(
        _rwkv_scan_kernel,
        grid=(2, n_seq // SCAN_SEQS, nc),
        in_specs=[ends, blk, blk],
        out_specs=[blk, ends],
        out_shape=[jax.ShapeDtypeStruct(m.shape, F32), jax.ShapeDtypeStruct((2, n_seq, rows, LANES), F32)],
        scratch_shapes=[pltpu.VMEM((SCAN_SEQS, rows, LANES), F32)],
        compiler_params=_params("parallel", "parallel", "arbitrary"),
        name="rwkv_state_scan",
    )(s0, m, nn)


OUT_CHUNKS = 4


def _rwkv_out_kernel(qf_ref, qb_ref, pf_ref, pb_ref, sf_ref, sb_ref, bonus_ref, g_ref, gw_ref, gb_ref, ones_ref,
                     o_ref, y_scr):
    for c in range(OUT_CHUNKS):
        tok = slice(c * CHUNK, (c + 1) * CHUNK)
        for p in range(PAIRS):
            ls = slice(p * LANES, (p + 1) * LANES)
            rs = slice(p * HEAD_DIM, (p + 1) * HEAD_DIM)
            y = pf_ref[tok, ls] + pb_ref[tok, ls]
            y = y + _dot_nt(qf_ref[tok, ls], _bd(sf_ref[c, rs, :].astype(BF16)))
            y = y + _dot_nt(qb_ref[tok, ls], _bd(sb_ref[c, rs, :].astype(BF16)))
            y_scr[tok, ls] = y
    y = y_scr[...]
    yc = y - _head_sum(y, ones_ref[...]) * (1.0 / HEAD_DIM)
    var = _head_sum(yc * yc, ones_ref[...]) * (1.0 / HEAD_DIM)
    yn = yc * lax.rsqrt(var + GN_EPS) * gw_ref[...] + gb_ref[...] + bonus_ref[...]
    o_ref[...] = (yn * g_ref[...]).astype(o_ref.dtype)


def _rwkv_out(qt, pv, starts, bonus, g, gn_w, gn_b, *, n_seq, seq_len):
    n = bonus.shape[0]
    ng = seq_len // (CHUNK * OUT_CHUNKS)
    tm = CHUNK * OUT_CHUNKS
    rows = PAIRS * HEAD_DIM
    tok = lambda d: pl.BlockSpec((None, tm, B_WIDTH), lambda b, c: (d, b * ng + c, 0))
    st = lambda d: pl.BlockSpec((None, None, OUT_CHUNKS, rows, LANES), lambda b, c: (d, b, c, 0, 0))
    one = pl.BlockSpec((tm, B_WIDTH), lambda b, c: (b * ng + c, 0))
    vec = pl.BlockSpec((1, B_WIDTH), lambda b, c: (0, 0))
    return pl.pallas_call(
        _rwkv_out_kernel,
        grid=(n_seq, ng),
        in_specs=[tok(0), tok(1), tok(0), tok(1), st(0), st(1), one, one, vec, vec,
                  pl.BlockSpec((2 * LANES, LANES), lambda b, c: (0, 0))],
        out_specs=one,
        out_shape=jax.ShapeDtypeStruct((n, B_WIDTH), BF16),
        scratch_shapes=[pltpu.VMEM((tm, B_WIDTH), F32)],
        compiler_params=_params("parallel", "parallel"),
        name="rwkv_output",
    )(qt, qt, pv, pv, starts, starts, bonus, g, gn_w.reshape(1, -1), gn_b.reshape(1, -1), _pair_ones())


def _to_pairs(s):
    lead = s.shape[:-3]
    s = s.reshape(*lead, PAIRS, 2, HEAD_DIM, HEAD_DIM)
    s = jnp.swapaxes(s, -3, -2)
    return s.reshape(*lead, PAIRS * HEAD_DIM, LANES)


def _from_pairs(s):
    lead = s.shape[:-2]
    s = s.reshape(*lead, PAIRS, HEAD_DIM, 2, HEAD_DIM)
    s = jnp.swapaxes(s, -3, -2)
    return s.reshape(*lead, B_HEADS, HEAD_DIM, HEAD_DIM)


def _rwkv_mixer(pb, s0, j, W, *, n_seq, seq_len):
    r, v, kk, g, bonus, lw, kd, ab = _rwkv_prep(pb, W, j, seq_len=seq_len)
    m, nn, qt, pv = _rwkv_chunks(r, v, kk, lw, kd, ab, n_seq=n_seq, seq_len=seq_len)
    starts, fin = _rwkv_scan(s0, m, nn)
    ob = _rwkv_out(qt, pv, starts, bonus, g, W["b_gn_w"][j], W["b_gn_b"][j], n_seq=n_seq, seq_len=seq_len)
    return ob, fin


def _forward(x, mod, caches, W, Wb, *, row0):
    n_seq, seq_len, _ = x.shape
    latent = caches is not None
    x = x.reshape(n_seq * seq_len, D_MODEL)
    rope = _rope_tables(seq_len) if latent else None
    kw = dict(seq_len=seq_len, row0=row0)
    norm = lambda l, s: W["norms"][l, s].reshape(1, D_MODEL)
    gf = W["final_norm"].reshape(1, D_MODEL)
    ak, av, bs, ck, cv = [], [], [], [], []
    for l in range(DEPTH):
        j = l // 2
        x = _ffn(x, mod[l], norm(l, 0), Wb["ffn_w1"][l, 0], Wb["ffn_w3"][l, 0], Wb["ffn_w2"][l, 0], gf, sub=0, **kw)
        if l % 2 == 0:
            pa, pb = _proj(x, mod[l], norm(l, 1), Wb["w_in_ab"][j], rope, sub=1, splits=(A_COLS, B_COLS),
                           dtypes=(BF16 if latent else F32, F32), n_rope=(A_Q + A_KV) if latent else 0, **kw)
            if latent:
                k_ctx = caches["a_k"][:, j].reshape(n_seq, -1, A_KV)
                v_ctx = caches["a_v"][:, j].reshape(n_seq, -1, A_KV)
                s0 = _to_pairs(jnp.swapaxes(caches["b_s"][:, j], 0, 1))
            else:
                k_ctx = v_ctx = None
                s0 = jnp.zeros((2, n_seq, PAIRS * HEAD_DIM, LANES), F32)
                ak.append(pa[:, A_Q:A_Q + A_KV].reshape(n_seq, seq_len, A_KV_HEADS, HEAD_DIM))
                av.append(pa[:, A_Q + A_KV:].reshape(n_seq, seq_len, A_KV_HEADS, HEAD_DIM))
            oa = _attn_a(pa, W["a_sink"][j], k_ctx, v_ctx, n_seq=n_seq, seq_len=seq_len)
            ob, fin = _rwkv_mixer(pb, s0, j, W, n_seq=n_seq, seq_len=seq_len)
            if not latent:
                bs.append(jnp.swapaxes(_from_pairs(fin), 0, 1))
            w_out = Wb["w_out_ab"][j]
            x = _outproj(x, mod[l], [oa, ob], [w_out[:A_Q], w_out[A_Q:]], sub=1, **kw)
        else:
            (pc,) = _proj(x, mod[l], norm(l, 1), Wb["w_in_c"][j], rope, sub=1, splits=(3 * C_WIDTH,),
                          dtypes=(BF16 if latent else F32,), n_rope=2 * C_WIDTH if latent else 0, **kw)
            if latent:
                k_ctx = caches["c_k"][:, j].reshape(n_seq, -1, C_WIDTH)
                v_ctx = caches["c_v"][:, j].reshape(n_seq, -1, C_WIDTH)
            else:
                k_ctx = v_ctx = None
                ck.append(pc[:, C_WIDTH:2 * C_WIDTH].reshape(n_seq, seq_len, C_HEADS, C_VDIM))
                cv.append(pc[:, 2 * C_WIDTH:].reshape(n_seq, seq_len, C_HEADS, C_VDIM))
            lam_vecs = jnp.stack([W["c_lq1"][j], W["c_lk1"][j], W["c_lq2"][j], W["c_lk2"][j]])
            oc = _attn_c(pc, lam_vecs, W["c_subln"][j].reshape(1, C_VDIM), k_ctx, v_ctx,
                         lam_init=0.8 - 0.6 * math.exp(-0.3 * l), n_seq=n_seq, seq_len=seq_len)
            x = _outproj(x, mod[l], [oc], [Wb["w_out_c"][j]], sub=1, **kw)
        x = _ffn(x, mod[l], norm(l, 2), Wb["ffn_w1"][l, 1], Wb["ffn_w3"][l, 1], Wb["ffn_w2"][l, 1], gf,
                 sub=2, final=(l == DEPTH - 1), **kw)
    return x.reshape(n_seq, seq_len, D_MODEL), (ak, av, bs, ck, cv)


def kernel(x_prompt, x_sample, c, c_ctx, cache_a_k, cache_a_v, state_b_wkv, cache_c_k, cache_c_v, norms, mod_w, mod_b, ffn_w1, ffn_w3, ffn_w2, w_in_ab, w_out_ab, a_sink, b_mu_prev, b_mu_next, b_w0, b_w2, b_a0, b_a2, b_k_k, b_k_a, b_r_k, b_g2, b_gn_w, b_gn_b, w_in_c, w_out_c, c_lq1, c_lk1, c_lq2, c_lk2, c_subln, final_norm):
    W = dict(norms=norms, a_sink=a_sink, b_mu_prev=b_mu_prev, b_mu_next=b_mu_next,
             b_w0=b_w0, b_w2=b_w2, b_a0=b_a0, b_a2=b_a2, b_k_k=b_k_k, b_k_a=b_k_a, b_r_k=b_r_k,
             b_g2=b_g2, b_gn_w=b_gn_w, b_gn_b=b_gn_b,
             c_lq1=c_lq1, c_lk1=c_lk1, c_lq2=c_lq2, c_lk2=c_lk2, c_subln=c_subln, final_norm=final_norm)
    Wb = dict(ffn_w1=ffn_w1.astype(BF16), ffn_w3=ffn_w3.astype(BF16), ffn_w2=ffn_w2.astype(BF16),
              w_in_ab=w_in_ab.astype(BF16), w_out_ab=w_out_ab.astype(BF16),
              w_in_c=w_in_c.astype(BF16), w_out_c=w_out_c.astype(BF16))
    n_lat = c.shape[0]
    assert 1 + n_lat <= MOD_ROWS
    cond = jnp.zeros((MOD_ROWS, D_MODEL), F32).at[0].set(c_ctx).at[1:1 + n_lat].set(c)
    mod = _modulation(cond, mod_w, mod_b)
    y_prompt, (ak, av, bs, ck, cv) = _forward(x_prompt, mod, None, W, Wb, row0=0)
    caches = dict(a_k=cache_a_k, a_v=cache_a_v, b_s=state_b_wkv, c_k=cache_c_k, c_v=cache_c_v)
    y_sample, _ = _forward(x_sample, mod, caches, W, Wb, row0=1)
    return (y_prompt, y_sample, jnp.stack(ak, axis=1), jnp.stack(av, axis=1), jnp.stack(bs, axis=1),
            jnp.stack(ck, axis=1), jnp.stack(cv, axis=1))
```

```python
import functools
import math

import numpy as np
import jax
import jax.numpy as jnp
from jax import lax
from jax.experimental import pallas as pl
from jax.experimental.pallas import tpu as pltpu

F32 = jnp.float32
BF16 = jnp.bfloat16
HIGHEST = lax.Precision.HIGHEST

D_MODEL = 1024
DEPTH = 4
GRID_W = 64
HEAD_DIM = 64
BLOCK = 128
A_HEADS = 8
A_KV_HEADS = 2
A_GROUP = A_HEADS // A_KV_HEADS
A_Q = A_HEADS * HEAD_DIM
A_KV = A_KV_HEADS * HEAD_DIM
A_COLS = A_Q + 2 * A_KV
B_HEADS = 8
B_WIDTH = B_HEADS * HEAD_DIM
DECAY_LORA = 64
AAA_LORA = 64
GATE_LORA = 128
B_COLS = 3 * B_WIDTH + 2 * DECAY_LORA + 2 * AAA_LORA + GATE_LORA
C_HEADS = 8
C_VDIM = 2 * HEAD_DIM
C_WIDTH = C_HEADS * C_VDIM
D_FF = 2816
N_MOD = 9
ROPE_THETA = 10000.0
NORM_EPS = 1e-6
GN_EPS = 64e-5
SUBLN_EPS = 1e-5

LANES = 128
MOD_ROWS = 16
CHUNK = 64
PAIRS = B_HEADS // 2
VMEM_LIMIT = 56 * 2 ** 20


def _params(*sem):
    return pltpu.CompilerParams(dimension_semantics=sem, vmem_limit_bytes=VMEM_LIMIT)


def _dot(a, b, precision=None):
    return jnp.dot(a, b, preferred_element_type=F32, precision=precision)


def _dot_nt(a, b, precision=None):
    return lax.dot_general(a, b, (((1,), (1,)), ((), ())), preferred_element_type=F32, precision=precision)


def _dot_tn(a, b, precision=None):
    return lax.dot_general(a, b, (((0,), (0,)), ((), ())), preferred_element_type=F32, precision=precision)


def _silu(x):
    return x * jax.nn.sigmoid(x)


def _modnorm(x, g, scale, shift):
    y = x * lax.rsqrt(jnp.mean(x * x, axis=-1, keepdims=True) + NORM_EPS)
    return (y * g) * (1.0 + scale) + shift


def _mod_kernel(c_ref, w_ref, b_ref, o_ref):
    o_ref[...] = _dot(_silu(c_ref[...]), w_ref[...], HIGHEST) + b_ref[...]


def _modulation(cond, mod_w, mod_b):
    n_col = N_MOD * D_MODEL
    tn = D_MODEL
    out = pl.pallas_call(
        _mod_kernel,
        grid=(DEPTH, n_col // tn),
        in_specs=[pl.BlockSpec((MOD_ROWS, D_MODEL), lambda l, j: (0, 0)),
                  pl.BlockSpec((None, D_MODEL, tn), lambda l, j: (l, 0, j)),
                  pl.BlockSpec((None, 1, tn), lambda l, j: (l, 0, j))],
        out_specs=pl.BlockSpec((None, MOD_ROWS, tn), lambda l, j: (l, 0, j)),
        out_shape=jax.ShapeDtypeStruct((DEPTH, MOD_ROWS, n_col), F32),
        compiler_params=_params("parallel", "parallel"),
        name="modulation",
    )(cond, mod_w, mod_b.reshape(DEPTH, 1, n_col))
    return out.reshape(DEPTH, MOD_ROWS, N_MOD, D_MODEL)


def _mod_spec(tm, seq_len, row0):
    if row0 == 0:
        return pl.BlockSpec((None, N_MOD, D_MODEL), lambda i: (0, 0, 0))
    return pl.BlockSpec((None, N_MOD, D_MODEL), lambda i: (row0 + (i * tm) // seq_len, 0, 0))


def _resident(shape):
    nd = len(shape)
    return pl.BlockSpec(shape, lambda i: (0,) * nd)


FF_TILE = 256


def _ffn_kernel(x_ref, mod_ref, g_ref, w1_ref, w3_ref, w2_ref, gf_ref, o_ref, act_ref, *, sub, final):
    x = x_ref[...]
    h = _modnorm(x, g_ref[...], mod_ref[3 * sub + 1:3 * sub + 2, :], mod_ref[3 * sub:3 * sub + 1, :]).astype(BF16)
    for j in range(D_FF // FF_TILE):
        cols = slice(j * FF_TILE, (j + 1) * FF_TILE)
        a = _dot(h, w1_ref[:, cols])
        b = _dot(h, w3_ref[:, cols])
        act_ref[:, cols] = (_silu(a) * b).astype(BF16)
    y = x + 0.5 * mod_ref[3 * sub + 2:3 * sub + 3, :] * _dot(act_ref[...], w2_ref[...])
    if final:
        y = y * lax.rsqrt(jnp.mean(y * y, axis=-1, keepdims=True) + NORM_EPS) * gf_ref[...]
    o_ref[...] = y


def _ffn(x, mod_l, g, w1, w3, w2, gf, *, sub, seq_len, row0, final=False):
    n = x.shape[0]
    tm = min(512, seq_len)
    return pl.pallas_call(
        functools.partial(_ffn_kernel, sub=sub, final=final),
        grid=(n // tm,),
        in_specs=[pl.BlockSpec((tm, D_MODEL), lambda i: (i, 0)),
                  _mod_spec(tm, seq_len, row0),
                  _resident((1, D_MODEL)),
                  _resident((D_MODEL, D_FF)),
                  _resident((D_MODEL, D_FF)),
                  _resident((D_FF, D_MODEL)),
                  _resident((1, D_MODEL))],
        out_specs=pl.BlockSpec((tm, D_MODEL), lambda i: (i, 0)),
        out_shape=jax.ShapeDtypeStruct((n, D_MODEL), F32),
        scratch_shapes=[pltpu.VMEM((tm, D_FF), BF16)],
        compiler_params=_params("parallel"),
        name="ffn",
    )(x, mod_l, g, w1, w3, w2, gf)


def _swap16(x):
    lane = lax.broadcasted_iota(jnp.int32, x.shape, 1)
    return jnp.where(lane % 32 < 16, pltpu.roll(x, LANES - 16, axis=1), pltpu.roll(x, 16, axis=1))


Q_SCALE = HEAD_DIM ** -0.5 * math.log2(math.e)


def _proj_kernel(*refs, sub, splits, n_rope, n_q):
    if n_rope:
        x_ref, mod_ref, g_ref, w_ref, cos_ref, sin_ref = refs[:6]
        o_refs = refs[6:]
    else:
        x_ref, mod_ref, g_ref, w_ref = refs[:4]
        o_refs = refs[4:]
    h = _modnorm(x_ref[...], g_ref[...], mod_ref[3 * sub + 1:3 * sub + 2, :], mod_ref[3 * sub:3 * sub + 1, :]).astype(BF16)
    start = 0
    for o_ref, width in zip(o_refs, splits):
        for c in range(0, width, 256):
            wd = min(256, width - c)
            y = _dot(h, w_ref[:, start + c:start + c + wd])
            for p in range(0, wd, LANES):
                yp = y[:, p:p + LANES]
                if start + c + p < n_q:
                    yp = yp * Q_SCALE
                if start + c + p < n_rope:
                    yp = yp * cos_ref[...] + _swap16(yp) * sin_ref[...]
                o_ref[:, c + p:c + p + LANES] = yp.astype(o_ref.dtype)
        start += width


def _proj(x, mod_l, g, w, rope, *, sub, splits, dtypes, n_rope, n_q, seq_len, row0):
    n = x.shape[0]
    tm = min(512, seq_len)
    n_col = w.shape[1]
    in_specs = [pl.BlockSpec((tm, D_MODEL), lambda i: (i, 0)),
                _mod_spec(tm, seq_len, row0),
                _resident((1, D_MODEL)),
                _resident((D_MODEL, n_col))]
    args = [x, mod_l, g, w]
    if n_rope:
        tiles = seq_len // tm
        in_specs += [pl.BlockSpec((tm, LANES), lambda i: (i % tiles, 0))] * 2
        args += list(rope)
    return pl.pallas_call(
        functools.partial(_proj_kernel, sub=sub, splits=splits, n_rope=n_rope, n_q=n_q),
        grid=(n // tm,),
        in_specs=in_specs,
        out_specs=[pl.BlockSpec((tm, wd), lambda i: (i, 0)) for wd in splits],
        out_shape=[jax.ShapeDtypeStruct((n, wd), dt) for wd, dt in zip(splits, dtypes)],
        compiler_params=_params("parallel"),
        name="mixer_in_proj",
    )(*args)


def _rope_tables(n_tok):
    t = jnp.arange(n_tok, dtype=jnp.int32)
    row = (t // GRID_W).astype(F32)
    col = (t % GRID_W).astype(F32)
    n_freq = HEAD_DIM // 4
    inv = jnp.power(ROPE_THETA, -jnp.arange(n_freq, dtype=F32) / n_freq)
    ar, ac = row[:, None] * inv, col[:, None] * inv
    cos = jnp.concatenate([jnp.cos(ar), jnp.cos(ar), jnp.cos(ac), jnp.cos(ac)], axis=1)
    sin = jnp.concatenate([-jnp.sin(ar), jnp.sin(ar), -jnp.sin(ac), jnp.sin(ac)], axis=1)
    return jnp.tile(cos, (1, 2)), jnp.tile(sin, (1, 2))


def _outproj_kernel(*refs, sub, n_in):
    x_ref, mod_ref = refs[:2]
    a_refs = refs[2:2 + n_in]
    w_refs = refs[2 + n_in:2 + 2 * n_in]
    o_ref = refs[-1]
    y = _dot(a_refs[0][...], w_refs[0][...])
    for a_ref, w_ref in zip(a_refs[1:], w_refs[1:]):
        y = y + _dot(a_ref[...], w_ref[...])
    o_ref[...] = x_ref[...] + mod_ref[3 * sub + 2:3 * sub + 3, :] * y


def _outproj(x, mod_l, acts, ws, *, sub, seq_len, row0):
    n = x.shape[0]
    tm = min(512, seq_len)
    return pl.pallas_call(
        functools.partial(_outproj_kernel, sub=sub, n_in=len(acts)),
        grid=(n // tm,),
        in_specs=([pl.BlockSpec((tm, D_MODEL), lambda i: (i, 0)), _mod_spec(tm, seq_len, row0)]
                  + [pl.BlockSpec((tm, a.shape[1]), lambda i: (i, 0)) for a in acts]
                  + [_resident(w.shape) for w in ws]),
        out_specs=pl.BlockSpec((tm, D_MODEL), lambda i: (i, 0)),
        out_shape=jax.ShapeDtypeStruct((n, D_MODEL), F32),
        compiler_params=_params("parallel"),
        name="mixer_out_proj",
    )(x, mod_l, *acts, *ws)


def _attn_a_kernel(*refs, windowed):
    if windowed:
        sink_ref, q_ref, kp_ref, kc_ref, kn_ref, vp_ref, vc_ref, vn_ref, kx_ref, vx_ref, o_ref, e_ref = refs
    else:
        sink_ref, q_ref, kx_ref, vx_ref, o_ref, e_ref = refs
    i = pl.program_id(1)
    nb = pl.num_programs(1)
    n_ctx = kx_ref.shape[0]
    lane = lax.broadcasted_iota(jnp.int32, (BLOCK, LANES), 1)

    q32 = q_ref[...].astype(F32)
    blocks = []
    for h in range(A_HEADS):
        g = h // A_GROUP
        col = q32[:, (h // 2) * LANES:(h // 2 + 1) * LANES]
        if h % 2 != g:
            col = pltpu.roll(col, HEAD_DIM, axis=1)
        blocks.append(jnp.where(lane // HEAD_DIM == g, col, 0.0).astype(BF16))
    qz = jnp.concatenate(blocks, axis=0)

    k_parts = [kx_ref[...].astype(BF16)]
    v_parts = [vx_ref[...].astype(BF16)]
    if windowed:
        k_parts += [r[...].astype(BF16) for r in (kp_ref, kc_ref, kn_ref)]
        v_parts += [r[...].astype(BF16) for r in (vp_ref, vc_ref, vn_ref)]
        row = lax.broadcasted_iota(jnp.int32, (BLOCK, BLOCK), 0)
        colk = lax.broadcasted_iota(jnp.int32, (BLOCK, BLOCK), 1)
        ok_p = jnp.logical_and(colk >= row, i > 0)
        ok_n = jnp.logical_and(colk <= row, i < nb - 1)
        zeros = jnp.zeros((BLOCK, BLOCK), F32)
        bias = jnp.concatenate([jnp.zeros((BLOCK, n_ctx), F32), jnp.where(ok_p, 0.0, -jnp.inf), zeros,
                                jnp.where(ok_n, 0.0, -jnp.inf)], axis=1)
    keys = jnp.concatenate(k_parts, axis=0) if windowed else k_parts[0]
    vals = jnp.concatenate(v_parts, axis=0) if windowed else v_parts[0]
    s_all = _dot_nt(qz, keys)
    dens = []
    for h in range(A_HEADS):
        rows = slice(h * BLOCK, (h + 1) * BLOCK)
        s = s_all[rows]
        if windowed:
            s = s + bias
        sink = sink_ref[h] * math.log2(math.e)
        m = jnp.maximum(jnp.max(s, axis=-1, keepdims=True), sink)
        e = jnp.exp2(s - m)
        dens.append(jnp.sum(e, axis=-1, keepdims=True) + jnp.exp2(sink - m))
        e_ref[rows, :] = e.astype(BF16)
    o_all = _dot(e_ref[...], vals)
    for c in range(A_HEADS // 2):
        halves = []
        for h in (2 * c, 2 * c + 1):
            o = o_all[h * BLOCK:(h + 1) * BLOCK] / dens[h]
            if h % 2 != h // A_GROUP:
                o = pltpu.roll(o, HEAD_DIM, axis=1)
            halves.append(o)
        o_ref[:, c * LANES:(c + 1) * LANES] = jnp.where(lane < HEAD_DIM, halves[0], halves[1]).astype(o_ref.dtype)


def _attn_a(pa, sink, k_ctx, v_ctx, *, n_seq, seq_len):
    n = pa.shape[0]
    nb = seq_len // BLOCK
    kcol, vcol = A_Q // A_KV, A_Q // A_KV + 1
    smem = pl.BlockSpec(memory_space=pltpu.SMEM)
    q_spec = pl.BlockSpec((BLOCK, A_Q), lambda b, i: (b * nb + i, 0))
    if k_ctx is not None:
        n_ctx = k_ctx.shape[1]

        def blk(colblk, off):
            return pl.BlockSpec((BLOCK, A_KV), lambda b, i: (b * nb + jnp.clip(i + off, 0, nb - 1), colblk))

        in_specs = [smem, q_spec, blk(kcol, -1), blk(kcol, 0), blk(kcol, 1), blk(vcol, -1), blk(vcol, 0), blk(vcol, 1),
                    pl.BlockSpec((None, n_ctx, A_KV), lambda b, i: (b, 0, 0)),
                    pl.BlockSpec((None, n_ctx, A_KV), lambda b, i: (b, 0, 0))]
        args = [sink, pa, pa, pa, pa, pa, pa, pa, k_ctx, v_ctx]
    else:
        in_specs = [smem, q_spec,
                    pl.BlockSpec((seq_len, A_KV), lambda b, i: (b, kcol)),
                    pl.BlockSpec((seq_len, A_KV), lambda b, i: (b, vcol))]
        args = [sink, pa, pa, pa]
        n_ctx = seq_len
    n_keys = n_ctx + (3 * BLOCK if k_ctx is not None else 0)
    return pl.pallas_call(
        functools.partial(_attn_a_kernel, windowed=k_ctx is not None),
        grid=(n_seq, nb),
        in_specs=in_specs,
        out_specs=pl.BlockSpec((BLOCK, A_Q), lambda b, i: (b * nb + i, 0)),
        out_shape=jax.ShapeDtypeStruct((n, A_Q), BF16),
        scratch_shapes=[pltpu.VMEM((A_HEADS * BLOCK, n_keys), BF16)],
        compiler_params=_params("parallel", "parallel"),
        name="mixer_a_attention",
    )(*args)


def _attn_c_kernel(*refs, lam_init, n_lat, n_ctx):
    lam_ref, subln_ref, q_ref = refs[:3]
    rest = list(refs[3:])
    kx_ref = vx_ref = kl_ref = vl_ref = None
    if n_ctx:
        kx_ref, vx_ref = rest[:2]
        rest = rest[2:]
    if n_lat:
        kl_ref, vl_ref = rest[:2]
        rest = rest[2:]
    o_ref, vt_ref, s_ref, e_ref = rest
    KEY_TILE = s_ref.shape[1]
    tiles = ([(kx_ref, vx_ref, t) for t in range(0, n_ctx, KEY_TILE)]
             + [(kl_ref, vl_ref, t) for t in range(0, n_lat, KEY_TILE)])
    tq = q_ref.shape[0]
    cols = [slice(i * KEY_TILE, (i + 1) * KEY_TILE) for i in range(len(tiles))]

    @pl.when(pl.program_id(2) == 0)
    def _():
        for i, (_, v_ref, t) in enumerate(tiles):
            vt_ref[:, cols[i]] = v_ref[t:t + KEY_TILE, :].astype(F32).T.astype(BF16)

    lam = (jnp.exp(jnp.sum(lam_ref[0:1, :] * lam_ref[1:2, :], axis=-1, keepdims=True))
           - jnp.exp(jnp.sum(lam_ref[2:3, :] * lam_ref[3:4, :], axis=-1, keepdims=True)) + lam_init)

    q = q_ref[...].astype(BF16)
    lane = lax.broadcasted_iota(jnp.int32, q.shape, 1)
    zero = jnp.zeros_like(q)
    qz = jnp.concatenate([jnp.where(lane < HEAD_DIM, q, zero), jnp.where(lane >= HEAD_DIM, q, zero)], axis=0)
    m = jnp.full((1, 2 * tq), -jnp.inf, F32)
    l = jnp.zeros((1, 2 * tq), F32)
    acc = jnp.zeros((C_VDIM, 2 * tq), F32)
    rb = 32

    def scores(i):
        k_ref, _, t = tiles[i]
        s = _dot_nt(k_ref[t:t + KEY_TILE, :].astype(BF16), qz)
        s_ref[i % 2] = s
        return jnp.max(s, axis=0, keepdims=True)

    top = scores(0)
    for i in range(len(tiles)):
        m_new = jnp.maximum(m, top)
        if i + 1 < len(tiles):
            top = scores(i + 1)
        corr = jnp.exp2(m - m_new)
        part = jnp.zeros((8, 2 * tq), F32)
        for r0 in range(0, KEY_TILE, rb):
            e = jnp.exp2(s_ref[i % 2, r0:r0 + rb, :] - m_new)
            for r1 in range(0, rb, 8):
                part = part + e[r1:r1 + 8]
            e_ref[i % 2, r0:r0 + rb, :] = e.astype(BF16)
        l = l * corr + jnp.sum(part, axis=0, keepdims=True)
        acc = acc * corr + _dot(vt_ref[:, cols[i]], e_ref[i % 2])
        m = m_new
    o = acc[:, :tq] / l[:, :tq] - lam * (acc[:, tq:] / l[:, tq:])
    on = o * lax.rsqrt(jnp.mean(o * o, axis=0, keepdims=True) + SUBLN_EPS)
    o_ref[...] = (on.T * subln_ref[...] * (1.0 - lam_init)).astype(o_ref.dtype)


def _attn_c(pc, lam_vecs, subln, k_ctx, v_ctx, *, lam_init, n_seq, seq_len):
    n = pc.shape[0]
    tq = min(512, seq_len)
    nq = seq_len // tq
    latent = k_ctx is not None
    in_specs = [pl.BlockSpec((4, HEAD_DIM), lambda b, h, i: (0, 0)),
                pl.BlockSpec((1, C_VDIM), lambda b, h, i: (0, 0)),
                pl.BlockSpec((tq, C_VDIM), lambda b, h, i: (b * nq + i, h))]
    args = [lam_vecs, subln, pc]
    own_k = pl.BlockSpec((seq_len, C_VDIM), lambda b, h, i: (b, C_HEADS + h))
    own_v = pl.BlockSpec((seq_len, C_VDIM), lambda b, h, i: (b, 2 * C_HEADS + h))
    if latent:
        n_ctx = k_ctx.shape[1]
        in_specs += [pl.BlockSpec((None, n_ctx, C_VDIM), lambda b, h, i: (b, 0, h))] * 2
        args += [k_ctx, v_ctx]
        n_lat = seq_len
    else:
        n_ctx, n_lat = seq_len, 0
    in_specs += [own_k, own_v]
    args += [pc, pc]
    key_tile = 512 if (n_ctx % 512 == 0 and n_lat % 512 == 0) else 256
    return pl.pallas_call(
        functools.partial(_attn_c_kernel, lam_init=lam_init, n_lat=n_lat, n_ctx=n_ctx),
        grid=(n_seq, C_HEADS, nq),
        in_specs=in_specs,
        out_specs=pl.BlockSpec((tq, C_VDIM), lambda b, h, i: (b * nq + i, h)),
        out_shape=jax.ShapeDtypeStruct((n, C_WIDTH), BF16),
        scratch_shapes=[pltpu.VMEM((C_VDIM, n_ctx + n_lat), BF16),
                        pltpu.VMEM((2, key_tile, 2 * tq), F32),
                        pltpu.VMEM((2, key_tile, 2 * tq), BF16)],
        compiler_params=_params("parallel", "parallel", "arbitrary"),
        name="mixer_c_attention",
    )(*args)


def _pair_ones():
    idx = np.arange(LANES) // HEAD_DIM
    ones = (idx[:, None] == idx[None, :]).astype(np.float32)
    return jnp.asarray(np.concatenate([ones, ones], axis=0), dtype=BF16)


def _head_sum(x, ones2):
    out = []
    for p in range(PAIRS):
        hi, lo = _split(x[:, p * LANES:(p + 1) * LANES])
        out.append(_dot(jnp.concatenate([hi, lo], axis=1), ones2))
    return jnp.concatenate(out, axis=1)


def _rwkv_prep_kernel(pb_ref, hp_ref, hn_ref, mup_ref, mun_ref, w0_ref, w2_ref, a0_ref, a2_ref,
                      kk_ref, ka_ref, rk_ref, g2_ref, bd_ref,
                      r_out, v_out, kkn_out, g_out, bonus_out, lw_out, kd_out, ab_out, *, tiles_per_seq):
    i = pl.program_id(0)
    tm = pb_ref.shape[0]
    first = i % tiles_per_seq == 0
    last = i % tiles_per_seq == tiles_per_seq - 1
    pb = pb_ref[...]
    rowi = lax.broadcasted_iota(jnp.int32, (tm, 1), 0)
    prev_row = jnp.where(first, 0.0, hp_ref[7:8, :])
    next_row = jnp.where(last, 0.0, hn_ref[0:1, :])
    prev = jnp.where(rowi == 0, prev_row, pltpu.roll(pb, 1, axis=0))
    nxt = jnp.where(rowi == tm - 1, next_row, pltpu.roll(pb, tm - 1, axis=0))
    ps = pb + mup_ref[...] * (prev - pb) + mun_ref[...] * (nxt - pb)

    o = 3 * B_WIDTH
    r = ps[:, 0:B_WIDTH]
    k = ps[:, B_WIDTH:2 * B_WIDTH]
    v = ps[:, 2 * B_WIDTH:o]
    w_dn = (ps[:, o:o + DECAY_LORA], ps[:, o + DECAY_LORA:o + 2 * DECAY_LORA])
    o += 2 * DECAY_LORA
    a_dn = (ps[:, o:o + AAA_LORA], ps[:, o + AAA_LORA:o + 2 * AAA_LORA])
    o += 2 * AAA_LORA
    gd = ps[:, o:o + GATE_LORA]

    ones2 = bd_ref[...]
    kk = k * kk_ref[...]
    kk = kk / jnp.maximum(jnp.sqrt(_head_sum(kk * kk, ones2)), 1e-12)
    r_out[...] = r
    v_out[...] = v
    kkn_out[...] = kk
    g_out[...] = _mm3(jax.nn.sigmoid(gd), g2_ref[...])
    bonus = None
    for d in range(2):
        wl = w0_ref[d:d + 1, :] + _mm3(jnp.tanh(w_dn[d]), w2_ref[d])
        w_log = -jax.nn.softplus(-wl) - 0.5
        lw_out[d] = -jnp.exp(w_log)
        a = jax.nn.sigmoid(a0_ref[d:d + 1, :] + _mm3(a_dn[d], a2_ref[d]))
        k_d = k * (1.0 + (a - 1.0) * ka_ref[...])
        kd_out[d] = k_d
        ab_out[d] = a * kk
        bo = _head_sum(r * k_d * rk_ref[...], ones2) * v
        bonus = bo if bonus is None else bonus + bo
    bonus_out[...] = bonus


def _rwkv_prep(pb, W, j, *, seq_len):
    n = pb.shape[0]
    tm = 256
    tps = seq_len // tm
    nblk8 = n // 8
    row = lambda a: a.reshape(1, -1)
    res = _resident
    in_specs = [pl.BlockSpec((tm, B_COLS), lambda i: (i, 0)),
                pl.BlockSpec((8, B_COLS), lambda i: (jnp.maximum(i * (tm // 8) - 1, 0), 0)),
                pl.BlockSpec((8, B_COLS), lambda i: (jnp.minimum((i + 1) * (tm // 8), nblk8 - 1), 0)),
                res((1, B_COLS)), res((1, B_COLS)),
                res((2, B_WIDTH)), res((2, DECAY_LORA, B_WIDTH)),
                res((2, B_WIDTH)), res((2, AAA_LORA, B_WIDTH)),
                res((1, B_WIDTH)), res((1, B_WIDTH)), res((1, B_WIDTH)),
                res((GATE_LORA, B_WIDTH)), res((2 * LANES, LANES))]
    one = pl.BlockSpec((tm, B_WIDTH), lambda i: (i, 0))
    two = pl.BlockSpec((2, tm, B_WIDTH), lambda i: (0, i, 0))
    s1 = jax.ShapeDtypeStruct((n, B_WIDTH), F32)
    s2 = jax.ShapeDtypeStruct((2, n, B_WIDTH), F32)
    return pl.pallas_call(
        functools.partial(_rwkv_prep_kernel, tiles_per_seq=tps),
        grid=(n // tm,),
        in_specs=in_specs,
        out_specs=[one, one, one, one, one, two, two, two],
        out_shape=[s1, s1, s1, s1, s1, s2, s2, s2],
        compiler_params=_params("parallel"),
        name="rwkv_prep",
    )(pb, pb, pb, row(W["b_mu_prev"][j]), row(W["b_mu_next"][j]), W["b_w0"][j], W["b_w2"][j],
      W["b_a0"][j], W["b_a2"][j], row(W["b_k_k"][j]), row(W["b_k_a"][j]), row(W["b_r_k"][j]),
      W["b_g2"][j], _pair_ones())


def _bd(x):
    lane = lax.broadcasted_iota(jnp.int32, x.shape, 1)
    return jnp.concatenate([jnp.where(lane < HEAD_DIM, x, 0.0), jnp.where(lane >= HEAD_DIM, x, 0.0)], axis=0)


def _pair_diag(full):
    lane = lax.broadcasted_iota(jnp.int32, (HEAD_DIM, LANES), 1)
    return jnp.where(lane < HEAD_DIM, full[:HEAD_DIM, :], full[HEAD_DIM:, :])


def _split(x):
    hi = x.astype(BF16)
    return hi, (x - hi.astype(F32)).astype(BF16)


def _lhs3(a):
    ah, al = _split(a)
    return jnp.concatenate([ah, al, ah], axis=1)


def _rhs3(b, bd, axis):
    bh, bl = _split(b)
    if bd:
        bh, bl = _bd(bh), _bd(bl)
    return jnp.concatenate([bh, bh, bl], axis=axis)


def _mm3(a, b, bd=False):
    bs = b if isinstance(b, (list, tuple)) else [b]
    rhs = [_rhs3(x, bd, 0) for x in bs]
    return _dot(_lhs3(a), rhs[0] if len(rhs) == 1 else jnp.concatenate(rhs, axis=1))


def _mm3_nt(a, bs):
    rhs = [_rhs3(x, True, 1) for x in bs]
    return _dot_nt(_lhs3(a), rhs[0] if len(rhs) == 1 else jnp.concatenate(rhs, axis=0))


def _mm3_tn(a, b):
    (ah, al), (bh, bl) = _split(a), _split(b)
    return _dot_tn(jnp.concatenate([ah, al, ah], axis=0), jnp.concatenate([bh, bh, bl], axis=0))


STEP_CHUNKS = 2


def _rwkv_chunk_kernel(r_ref, v_ref, kk_ref, lw_ref, kd_ref, ab_ref, m_out, n_out, qt_out, pv_out):
    d = pl.program_id(0)
    row = lax.broadcasted_iota(jnp.int32, (CHUNK, LANES), 0)
    col = lax.broadcasted_iota(jnp.int32, (CHUNK, LANES), 1) % HEAD_DIM
    ahead = jnp.where(d == 0, row - col, col - row)
    strict = ahead > 0
    incl = ahead >= 0
    eye = (col == row).astype(F32)
    tri = jnp.where(incl[:, :CHUNK], 1.0, 0.0).astype(BF16)
    tri3 = jnp.concatenate([tri, tri, tri], axis=1)
    H = CHUNK
    units = [(g, p) for g in range(STEP_CHUNKS) for p in range(PAIRS)]
    at = [(slice(g * CHUNK, (g + 1) * CHUNK), slice(p * LANES, (p + 1) * LANES)) for g, p in units]
    st = [(g, slice(p * HEAD_DIM, (p + 1) * HEAD_DIM)) for g, p in units]

    lw = [lw_ref[t, ls] for t, ls in at]
    cum = []
    for x in lw:
        hi = x.astype(BF16)
        r1 = x - hi.astype(F32)
        mid = r1.astype(BF16)
        lo = (r1 - mid.astype(F32)).astype(BF16)
        cum.append(_dot(tri3, jnp.concatenate([hi, mid, lo], axis=0)))
    w_inv = [jnp.exp(-c) for c in cum]
    w_all = [jnp.exp(jnp.sum(x, axis=0, keepdims=True)) for x in lw]
    alpha = [jnp.exp(c - x) * kk_ref[t, ls] for c, x, (t, ls) in zip(cum, lw, at)]
    beta = [ab_ref[t, ls] * w for w, (t, ls) in zip(w_inv, at)]
    kappa = [kd_ref[t, ls] * w for w, (t, ls) in zip(w_inv, at)]
    rho = [jnp.exp(c) * r_ref[t, ls] for c, (t, ls) in zip(cum, at)]

    cross = [_mm3_nt(jnp.concatenate([a, r], axis=0), [b, k]) for a, r, b, k in zip(alpha, rho, beta, kappa)]
    l_ab = [jnp.where(strict, y[:H, :LANES], 0.0) for y in cross]
    l_rb = [jnp.where(incl, y[H:, :LANES], 0.0) for y in cross]
    l_ak = [jnp.where(strict, y[:H, LANES:], 0.0) for y in cross]
    l_rk = [jnp.where(incl, y[H:, LANES:], 0.0) for y in cross]

    pw = [-x for x in l_ab]
    t_inv = [eye + x for x in pw]
    pw = [_mm3(x, x, bd=True) for x in pw]
    for _ in range(int(math.log2(CHUNK)) - 2):
        both = [_mm3(jnp.concatenate([t, x], axis=0), x, bd=True) for t, x in zip(t_inv, pw)]
        t_inv = [t + y[:H] for t, y in zip(t_inv, both)]
        pw = [y[H:] for y in both]
    t_inv = [t + _mm3(t, x, bd=True) for t, x in zip(t_inv, pw)]

    xs = [_mm3(t, [a, x], bd=True) for t, a, x in zip(t_inv, alpha, l_ak)]
    xa = [x[:, :LANES] for x in xs]
    xk = [x[:, LANES:] for x in xs]
    lx = [_mm3(lb, [a, k], bd=True) for lb, a, k in zip(l_rb, xa, xk)]
    for u, (t, ls) in enumerate(at):
        qt_out[t, ls] = (rho[u] - lx[u][:, :LANES]).astype(qt_out.dtype)
    for u, (t, ls) in enumerate(at):
        pm = l_rk[u] - lx[u][:, LANES:]
        pv_out[t, ls] = _dot(pm.astype(BF16), _bd(v_ref[t, ls].astype(BF16)))
    g = [_mm3_tn(x, b) for x, b in zip(xs, beta)]
    for u, (c, rs) in enumerate(st):
        m_out[c, rs, :] = (eye - _pair_diag(g[u][:LANES])) * w_all[u]
    z = [k - _pair_diag(y[LANES:]) for k, y in zip(kappa, g)]
    for u, ((t, ls), (c, rs)) in enumerate(zip(at, st)):
        n_out[c, rs, :] = _pair_diag(_mm3_tn(v_ref[t, ls], z[u])) * w_all[u]


def _rwkv_chunks(r, v, kk, lw, kd, ab, *, n_seq, seq_len):
    n = r.shape[0]
    nc = seq_len // CHUNK
    ng = nc // STEP_CHUNKS
    tm = CHUNK * STEP_CHUNKS
    one = pl.BlockSpec((tm, B_WIDTH), lambda d, b, c: (b * ng + c, 0))
    two = pl.BlockSpec((None, tm, B_WIDTH), lambda d, b, c: (d, b * ng + c, 0))
    st = pl.BlockSpec((None, None, STEP_CHUNKS, PAIRS * HEAD_DIM, LANES), lambda d, b, c: (d, b, c, 0, 0))
    st_shape = jax.ShapeDtypeStruct((2, n_seq, nc, PAIRS * HEAD_DIM, LANES), F32)
    return pl.pallas_call(
        _rwkv_chunk_kernel,
        grid=(2, n_seq, ng),
        in_specs=[one, one, one, two, two, two],
        out_specs=[st, st, two, two],
        out_shape=[st_shape, st_shape, jax.ShapeDtypeStruct((2, n, B_WIDTH), BF16),
                   jax.ShapeDtypeStruct((2, n, B_WIDTH), F32)],
        compiler_params=_params("parallel", "parallel", "parallel"),
        name="rwkv_chunk_operators",
    )(r, v, kk, lw, kd, ab)


def _rwkv_scan_kernel(s0_ref, m_ref, n_ref, start_out, fin_out, s_scr):
    c = pl.program_id(2)
    SCAN_SEQS = s_scr.shape[0]

    @pl.when(c == 0)
    def _():
        s_scr[...] = s0_ref[...]

    for b in range(SCAN_SEQS):
        for p in range(PAIRS):
            rs = slice(p * HEAD_DIM, (p + 1) * HEAD_DIM)
            s = s_scr[b, rs, :]
            start_out[b, rs, :] = s
            s_scr[b, rs, :] = _mm3(s, m_ref[b, rs, :], bd=True) + n_ref[b, rs, :]

    @pl.when(c == pl.num_programs(2) - 1)
    def _():
        fin_out[...] = s_scr[...]


def _rwkv_scan(s0, m, nn):
    _, n_seq, nc, rows, _ = m.shape
    SCAN_SEQS = min(8, n_seq)
    chunk_of = lambda d, c: c + d * (nc - 1 - 2 * c)
    blk = pl.BlockSpec((None, SCAN_SEQS, None, rows, LANES), lambda d, g, c: (d, g, chunk_of(d, c), 0, 0))
    ends = pl.BlockSpec((None, SCAN_SEQS, rows, LANES), lambda d, g, c: (d, g, 0, 0))
    return pl.pallas_call(
        _rwkv_scan_kernel,
        grid=(2, n_seq // SCAN_SEQS, nc),
        in_specs=[ends, blk, blk],
        out_specs=[blk, ends],
        out_shape=[jax.ShapeDtypeStruct(m.shape, F32), jax.ShapeDtypeStruct((2, n_seq, rows, LANES), F32)],
        scratch_shapes=[pltpu.VMEM((SCAN_SEQS, rows, LANES), F32)],
        compiler_params=_params("parallel", "parallel", "arbitrary"),
        name="rwkv_state_scan",
    )(s0, m, nn)


OUT_CHUNKS = 4


def _rwkv_out_kernel(qf_ref, qb_ref, pf_ref, pb_ref, sf_ref, sb_ref, bonus_ref, g_ref, gw_ref, gb_ref, ones_ref,
                     o_ref, y_scr):
    for c in range(OUT_CHUNKS):
        tok = slice(c * CHUNK, (c + 1) * CHUNK)
        for p in range(PAIRS):
            ls = slice(p * LANES, (p + 1) * LANES)
            rs = slice(p * HEAD_DIM, (p + 1) * HEAD_DIM)
            y = pf_ref[tok, ls] + pb_ref[tok, ls]
            y = y + _dot_nt(qf_ref[tok, ls], _bd(sf_ref[c, rs, :].astype(BF16)))
            y = y + _dot_nt(qb_ref[tok, ls], _bd(sb_ref[c, rs, :].astype(BF16)))
            y_scr[tok, ls] = y
    y = y_scr[...]
    yc = y - _head_sum(y, ones_ref[...]) * (1.0 / HEAD_DIM)
    var = _head_sum(yc * yc, ones_ref[...]) * (1.0 / HEAD_DIM)
    yn = yc * lax.rsqrt(var + GN_EPS) * gw_ref[...] + gb_ref[...] + bonus_ref[...]
    o_ref[...] = (yn * g_ref[...]).astype(o_ref.dtype)


def _rwkv_out(qt, pv, starts, bonus, g, gn_w, gn_b, *, n_seq, seq_len):
    n = bonus.shape[0]
    ng = seq_len // (CHUNK * OUT_CHUNKS)
    tm = CHUNK * OUT_CHUNKS
    rows = PAIRS * HEAD_DIM
    tok = lambda d: pl.BlockSpec((None, tm, B_WIDTH), lambda b, c: (d, b * ng + c, 0))
    st = lambda d: pl.BlockSpec((None, None, OUT_CHUNKS, rows, LANES), lambda b, c: (d, b, c, 0, 0))
    one = pl.BlockSpec((tm, B_WIDTH), lambda b, c: (b * ng + c, 0))
    vec = pl.BlockSpec((1, B_WIDTH), lambda b, c: (0, 0))
    return pl.pallas_call(
        _rwkv_out_kernel,
        grid=(n_seq, ng),
        in_specs=[tok(0), tok(1), tok(0), tok(1), st(0), st(1), one, one, vec, vec,
                  pl.BlockSpec((2 * LANES, LANES), lambda b, c: (0, 0))],
        out_specs=one,
        out_shape=jax.ShapeDtypeStruct((n, B_WIDTH), BF16),
        scratch_shapes=[pltpu.VMEM((tm, B_WIDTH), F32)],
        compiler_params=_params("parallel", "parallel"),
        name="rwkv_output",
    )(qt, qt, pv, pv, starts, starts, bonus, g, gn_w.reshape(1, -1), gn_b.reshape(1, -1), _pair_ones())


def _to_pairs(s):
    lead = s.shape[:-3]
    s = s.reshape(*lead, PAIRS, 2, HEAD_DIM, HEAD_DIM)
    s = jnp.swapaxes(s, -3, -2)
    return s.reshape(*lead, PAIRS * HEAD_DIM, LANES)


def _from_pairs(s):
    lead = s.shape[:-2]
    s = s.reshape(*lead, PAIRS, HEAD_DIM, 2, HEAD_DIM)
    s = jnp.swapaxes(s, -3, -2)
    return s.reshape(*lead, B_HEADS, HEAD_DIM, HEAD_DIM)


def _rwkv_mixer(pb, s0, j, W, *, n_seq, seq_len):
    r, v, kk, g, bonus, lw, kd, ab = _rwkv_prep(pb, W, j, seq_len=seq_len)
    m, nn, qt, pv = _rwkv_chunks(r, v, kk, lw, kd, ab, n_seq=n_seq, seq_len=seq_len)
    starts, fin = _rwkv_scan(s0, m, nn)
    ob = _rwkv_out(qt, pv, starts, bonus, g, W["b_gn_w"][j], W["b_gn_b"][j], n_seq=n_seq, seq_len=seq_len)
    return ob, fin


def _forward(x, mod, caches, W, Wb, *, row0):
    n_seq, seq_len, _ = x.shape
    latent = caches is not None
    x = x.reshape(n_seq * seq_len, D_MODEL)
    rope = _rope_tables(seq_len) if latent else None
    kw = dict(seq_len=seq_len, row0=row0)
    norm = lambda l, s: W["norms"][l, s].reshape(1, D_MODEL)
    gf = W["final_norm"].reshape(1, D_MODEL)
    ak, av, bs, ck, cv = [], [], [], [], []
    for l in range(DEPTH):
        j = l // 2
        x = _ffn(x, mod[l], norm(l, 0), Wb["ffn_w1"][l, 0], Wb["ffn_w3"][l, 0], Wb["ffn_w2"][l, 0], gf, sub=0, **kw)
        if l % 2 == 0:
            pa, pb = _proj(x, mod[l], norm(l, 1), Wb["w_in_ab"][j], rope, sub=1, splits=(A_COLS, B_COLS),
                           dtypes=(BF16 if latent else F32, F32), n_rope=(A_Q + A_KV) if latent else 0,
                           n_q=A_Q, **kw)
            if latent:
                k_ctx = caches["a_k"][:, j].reshape(n_seq, -1, A_KV)
                v_ctx = caches["a_v"][:, j].reshape(n_seq, -1, A_KV)
                s0 = _to_pairs(jnp.swapaxes(caches["b_s"][:, j], 0, 1))
            else:
                k_ctx = v_ctx = None
                s0 = jnp.zeros((2, n_seq, PAIRS * HEAD_DIM, LANES), F32)
                ak.append(pa[:, A_Q:A_Q + A_KV].reshape(n_seq, seq_len, A_KV_HEADS, HEAD_DIM))
                av.append(pa[:, A_Q + A_KV:].reshape(n_seq, seq_len, A_KV_HEADS, HEAD_DIM))
            oa = _attn_a(pa, W["a_sink"][j], k_ctx, v_ctx, n_seq=n_seq, seq_len=seq_len)
            ob, fin = _rwkv_mixer(pb, s0, j, W, n_seq=n_seq, seq_len=seq_len)
            if not latent:
                bs.append(jnp.swapaxes(_from_pairs(fin), 0, 1))
            w_out = Wb["w_out_ab"][j]
            x = _outproj(x, mod[l], [oa, ob], [w_out[:A_Q], w_out[A_Q:]], sub=1, **kw)
        else:
            (pc,) = _proj(x, mod[l], norm(l, 1), Wb["w_in_c"][j], rope, sub=1, splits=(3 * C_WIDTH,),
                          dtypes=(BF16 if latent else F32,), n_rope=2 * C_WIDTH if latent else 0,
                          n_q=C_WIDTH, **kw)
            if latent:
                k_ctx = caches["c_k"][:, j].reshape(n_seq, -1, C_WIDTH)
                v_ctx = caches["c_v"][:, j].reshape(n_seq, -1, C_WIDTH)
            else:
                k_ctx = v_ctx = None
                ck.append(pc[:, C_WIDTH:2 * C_WIDTH].reshape(n_seq, seq_len, C_HEADS, C_VDIM))
                cv.append(pc[:, 2 * C_WIDTH:].reshape(n_seq, seq_len, C_HEADS, C_VDIM))
            lam_vecs = jnp.stack([W["c_lq1"][j], W["c_lk1"][j], W["c_lq2"][j], W["c_lk2"][j]])
            oc = _attn_c(pc, lam_vecs, W["c_subln"][j].reshape(1, C_VDIM), k_ctx, v_ctx,
                         lam_init=0.8 - 0.6 * math.exp(-0.3 * l), n_seq=n_seq, seq_len=seq_len)
            x = _outproj(x, mod[l], [oc], [Wb["w_out_c"][j]], sub=1, **kw)
        x = _ffn(x, mod[l], norm(l, 2), Wb["ffn_w1"][l, 1], Wb["ffn_w3"][l, 1], Wb["ffn_w2"][l, 1], gf,
                 sub=2, final=(l == DEPTH - 1), **kw)
    return x.reshape(n_seq, seq_len, D_MODEL), (ak, av, bs, ck, cv)


def kernel(x_prompt, x_sample, c, c_ctx, cache_a_k, cache_a_v, state_b_wkv, cache_c_k, cache_c_v, norms, mod_w, mod_b, ffn_w1, ffn_w3, ffn_w2, w_in_ab, w_out_ab, a_sink, b_mu_prev, b_mu_next, b_w0, b_w2, b_a0, b_a2, b_k_k, b_k_a, b_r_k, b_g2, b_gn_w, b_gn_b, w_in_c, w_out_c, c_lq1, c_lk1, c_lq2, c_lk2, c_subln, final_norm):
    W = dict(norms=norms, a_sink=a_sink, b_mu_prev=b_mu_prev, b_mu_next=b_mu_next,
             b_w0=b_w0, b_w2=b_w2, b_a0=b_a0, b_a2=b_a2, b_k_k=b_k_k, b_k_a=b_k_a, b_r_k=b_r_k,
             b_g2=b_g2, b_gn_w=b_gn_w, b_gn_b=b_gn_b,
             c_lq1=c_lq1, c_lk1=c_lk1, c_lq2=c_lq2, c_lk2=c_lk2, c_subln=c_subln, final_norm=final_norm)
    Wb = dict(ffn_w1=ffn_w1.astype(BF16), ffn_w3=ffn_w3.astype(BF16), ffn_w2=ffn_w2.astype(BF16),
              w_in_ab=w_in_ab.astype(BF16), w_out_ab=w_out_ab.astype(BF16),
              w_in_c=w_in_c.astype(BF16), w_out_c=w_out_c.astype(BF16))
    n_lat = c.shape[0]
    assert 1 + n_lat <= MOD_ROWS
    cond = jnp.zeros((MOD_ROWS, D_MODEL), F32).at[0].set(c_ctx).at[1:1 + n_lat].set(c)
    mod = _modulation(cond, mod_w, mod_b)
    y_prompt, (ak, av, bs, ck, cv) = _forward(x_prompt, mod, None, W, Wb, row0=0)
    caches = dict(a_k=cache_a_k, a_v=cache_a_v, b_s=state_b_wkv, c_k=cache_c_k, c_v=cache_c_v)
    y_sample, _ = _forward(x_sample, mod, caches, W, Wb, row0=1)
    return (y_prompt, y_sample, jnp.stack(ak, axis=1), jnp.stack(av, axis=1), jnp.stack(bs, axis=1),
            jnp.stack(ck, axis=1), jnp.stack(cv, axis=1))
```

```python
import functools
import math

import numpy as np
import jax
import jax.numpy as jnp
from jax import lax
from jax.experimental import pallas as pl
from jax.experimental.pallas import tpu as pltpu

F32 = jnp.float32
BF16 = jnp.bfloat16
HIGHEST = lax.Precision.HIGHEST

D_MODEL = 1024
DEPTH = 4
GRID_W = 64
HEAD_DIM = 64
BLOCK = 128
A_HEADS = 8
A_KV_HEADS = 2
A_GROUP = A_HEADS // A_KV_HEADS
A_Q = A_HEADS * HEAD_DIM
A_KV = A_KV_HEADS * HEAD_DIM
A_COLS = A_Q + 2 * A_KV
B_HEADS = 8
B_WIDTH = B_HEADS * HEAD_DIM
DECAY_LORA = 64
AAA_LORA = 64
GATE_LORA = 128
B_COLS = 3 * B_WIDTH + 2 * DECAY_LORA + 2 * AAA_LORA + GATE_LORA
C_HEADS = 8
C_VDIM = 2 * HEAD_DIM
C_WIDTH = C_HEADS * C_VDIM
D_FF = 2816
N_MOD = 9
ROPE_THETA = 10000.0
NORM_EPS = 1e-6
GN_EPS = 64e-5
SUBLN_EPS = 1e-5

LANES = 128
MOD_ROWS = 16
CHUNK = 64
PAIRS = B_HEADS // 2
VMEM_LIMIT = 56 * 2 ** 20


def _params(*sem):
    return pltpu.CompilerParams(dimension_semantics=sem, vmem_limit_bytes=VMEM_LIMIT)


def _dot(a, b, precision=None):
    return jnp.dot(a, b, preferred_element_type=F32, precision=precision)


def _dot_nt(a, b, precision=None):
    return lax.dot_general(a, b, (((1,), (1,)), ((), ())), preferred_element_type=F32, precision=precision)


def _dot_tn(a, b, precision=None):
    return lax.dot_general(a, b, (((0,), (0,)), ((), ())), preferred_element_type=F32, precision=precision)


def _silu(x):
    return x * jax.nn.sigmoid(x)


def _modnorm(x, g, scale, shift):
    y = x * lax.rsqrt(jnp.mean(x * x, axis=-1, keepdims=True) + NORM_EPS)
    return (y * g) * (1.0 + scale) + shift


def _mod_kernel(c_ref, w_ref, b_ref, o_ref):
    o_ref[...] = _dot(_silu(c_ref[...]), w_ref[...], HIGHEST) + b_ref[...]


def _modulation(cond, mod_w, mod_b):
    n_col = N_MOD * D_MODEL
    tn = D_MODEL
    out = pl.pallas_call(
        _mod_kernel,
        grid=(DEPTH, n_col // tn),
        in_specs=[pl.BlockSpec((MOD_ROWS, D_MODEL), lambda l, j: (0, 0)),
                  pl.BlockSpec((None, D_MODEL, tn), lambda l, j: (l, 0, j)),
                  pl.BlockSpec((None, 1, tn), lambda l, j: (l, 0, j))],
        out_specs=pl.BlockSpec((None, MOD_ROWS, tn), lambda l, j: (l, 0, j)),
        out_shape=jax.ShapeDtypeStruct((DEPTH, MOD_ROWS, n_col), F32),
        compiler_params=_params("parallel", "parallel"),
        name="modulation",
    )(cond, mod_w, mod_b.reshape(DEPTH, 1, n_col))
    return out.reshape(DEPTH, MOD_ROWS, N_MOD, D_MODEL)


def _mod_spec(tm, seq_len, row0):
    if row0 == 0:
        return pl.BlockSpec((None, N_MOD, D_MODEL), lambda i: (0, 0, 0))
    return pl.BlockSpec((None, N_MOD, D_MODEL), lambda i: (row0 + (i * tm) // seq_len, 0, 0))


def _resident(shape):
    nd = len(shape)
    return pl.BlockSpec(shape, lambda i: (0,) * nd, pipeline_mode=pl.Buffered(1))


FF_TILE = 256


def _ffn_kernel(*refs, sub, final, n_mix):
    x_ref, mod_ref, g_ref, w1_ref, w3_ref, w2_ref, gf_ref = refs[:7]
    mix_refs = refs[7:7 + 2 * n_mix]
    o_ref, act_ref = refs[7 + 2 * n_mix:]
    x = x_ref[...]
    if n_mix:
        y = _dot(mix_refs[0][...], mix_refs[n_mix][...])
        for a_ref, w_ref in zip(mix_refs[1:n_mix], mix_refs[n_mix + 1:]):
            y = y + _dot(a_ref[...], w_ref[...])
        x = x + mod_ref[5:6, :] * y
    h = _modnorm(x, g_ref[...], mod_ref[3 * sub + 1:3 * sub + 2, :], mod_ref[3 * sub:3 * sub + 1, :]).astype(BF16)
    for j in range(D_FF // FF_TILE):
        cols = slice(j * FF_TILE, (j + 1) * FF_TILE)
        a = _dot(h, w1_ref[:, cols])
        b = _dot(h, w3_ref[:, cols])
        act_ref[:, cols] = (_silu(a) * b).astype(BF16)
    y = x + 0.5 * mod_ref[3 * sub + 2:3 * sub + 3, :] * _dot(act_ref[...], w2_ref[...])
    if final:
        y = y * lax.rsqrt(jnp.mean(y * y, axis=-1, keepdims=True) + NORM_EPS) * gf_ref[...]
    o_ref[...] = y


def _ffn(x, mod_l, g, w1, w3, w2, gf, *, sub, seq_len, row0, final=False, mix_acts=(), mix_ws=()):
    n = x.shape[0]
    tm = min(512, seq_len)
    return pl.pallas_call(
        functools.partial(_ffn_kernel, sub=sub, final=final, n_mix=len(mix_acts)),
        grid=(n // tm,),
        in_specs=([pl.BlockSpec((tm, D_MODEL), lambda i: (i, 0)),
                   _mod_spec(tm, seq_len, row0),
                   _resident((1, D_MODEL)),
                   _resident((D_MODEL, D_FF)),
                   _resident((D_MODEL, D_FF)),
                   _resident((D_FF, D_MODEL)),
                   _resident((1, D_MODEL))]
                  + [pl.BlockSpec((tm, a.shape[1]), lambda i: (i, 0)) for a in mix_acts]
                  + [_resident(w.shape) for w in mix_ws]),
        out_specs=pl.BlockSpec((tm, D_MODEL), lambda i: (i, 0)),
        out_shape=jax.ShapeDtypeStruct((n, D_MODEL), F32),
        scratch_shapes=[pltpu.VMEM((tm, D_FF), BF16)],
        compiler_params=_params("parallel"),
        name="ffn",
    )(x, mod_l, g, w1, w3, w2, gf, *mix_acts, *mix_ws)


def _swap16(x):
    lane = lax.broadcasted_iota(jnp.int32, x.shape, 1)
    return jnp.where(lane % 32 < 16, pltpu.roll(x, LANES - 16, axis=1), pltpu.roll(x, 16, axis=1))


Q_SCALE = HEAD_DIM ** -0.5 * math.log2(math.e)


def _proj_kernel(*refs, sub, splits, n_rope, n_q):
    if n_rope:
        x_ref, mod_ref, g_ref, w_ref, cos_ref, sin_ref = refs[:6]
        o_refs = refs[6:]
    else:
        x_ref, mod_ref, g_ref, w_ref = refs[:4]
        o_refs = refs[4:]
    h = _modnorm(x_ref[...], g_ref[...], mod_ref[3 * sub + 1:3 * sub + 2, :], mod_ref[3 * sub:3 * sub + 1, :]).astype(BF16)
    start = 0
    for o_ref, width in zip(o_refs, splits):
        for c in range(0, width, 256):
            wd = min(256, width - c)
            y = _dot(h, w_ref[:, start + c:start + c + wd])
            for p in range(0, wd, LANES):
                yp = y[:, p:p + LANES]
                if start + c + p < n_q:
                    yp = yp * Q_SCALE
                if start + c + p < n_rope:
                    yp = yp * cos_ref[...] + _swap16(yp) * sin_ref[...]
                o_ref[:, c + p:c + p + LANES] = yp.astype(o_ref.dtype)
        start += width


def _proj(x, mod_l, g, w, rope, *, sub, splits, dtypes, n_rope, n_q, seq_len, row0):
    n = x.shape[0]
    tm = min(512, seq_len)
    n_col = w.shape[1]
    in_specs = [pl.BlockSpec((tm, D_MODEL), lambda i: (i, 0)),
                _mod_spec(tm, seq_len, row0),
                _resident((1, D_MODEL)),
                _resident((D_MODEL, n_col))]
    args = [x, mod_l, g, w]
    if n_rope:
        tiles = seq_len // tm
        in_specs += [pl.BlockSpec((tm, LANES), lambda i: (i % tiles, 0))] * 2
        args += list(rope)
    return pl.pallas_call(
        functools.partial(_proj_kernel, sub=sub, splits=splits, n_rope=n_rope, n_q=n_q),
        grid=(n // tm,),
        in_specs=in_specs,
        out_specs=[pl.BlockSpec((tm, wd), lambda i: (i, 0)) for wd in splits],
        out_shape=[jax.ShapeDtypeStruct((n, wd), dt) for wd, dt in zip(splits, dtypes)],
        compiler_params=_params("parallel"),
        name="mixer_in_proj",
    )(*args)


def _rope_tables(n_tok):
    t = jnp.arange(n_tok, dtype=jnp.int32)
    row = (t // GRID_W).astype(F32)
    col = (t % GRID_W).astype(F32)
    n_freq = HEAD_DIM // 4
    inv = jnp.power(ROPE_THETA, -jnp.arange(n_freq, dtype=F32) / n_freq)
    ar, ac = row[:, None] * inv, col[:, None] * inv
    cos = jnp.concatenate([jnp.cos(ar), jnp.cos(ar), jnp.cos(ac), jnp.cos(ac)], axis=1)
    sin = jnp.concatenate([-jnp.sin(ar), jnp.sin(ar), -jnp.sin(ac), jnp.sin(ac)], axis=1)
    return jnp.tile(cos, (1, 2)), jnp.tile(sin, (1, 2))


def _attn_a_kernel(*refs, windowed):
    if windowed:
        sink_ref, q_ref, kp_ref, kc_ref, kn_ref, vp_ref, vc_ref, vn_ref, kx_ref, vx_ref, o_ref, e_ref = refs
    else:
        sink_ref, q_ref, kx_ref, vx_ref, o_ref, e_ref = refs
    i = pl.program_id(1)
    nb = pl.num_programs(1)
    n_ctx = kx_ref.shape[0]
    lane = lax.broadcasted_iota(jnp.int32, (BLOCK, LANES), 1)

    q32 = q_ref[...].astype(F32)
    blocks = []
    for h in range(A_HEADS):
        g = h // A_GROUP
        col = q32[:, (h // 2) * LANES:(h // 2 + 1) * LANES]
        if h % 2 != g:
            col = pltpu.roll(col, HEAD_DIM, axis=1)
        blocks.append(jnp.where(lane // HEAD_DIM == g, col, 0.0).astype(BF16))
    qz = jnp.concatenate(blocks, axis=0)

    k_parts = [kx_ref[...].astype(BF16)]
    v_parts = [vx_ref[...].astype(BF16)]
    if windowed:
        k_parts += [r[...].astype(BF16) for r in (kp_ref, kc_ref, kn_ref)]
        v_parts += [r[...].astype(BF16) for r in (vp_ref, vc_ref, vn_ref)]
        row = lax.broadcasted_iota(jnp.int32, (BLOCK, BLOCK), 0)
        colk = lax.broadcasted_iota(jnp.int32, (BLOCK, BLOCK), 1)
        ok_p = jnp.logical_and(colk >= row, i > 0)
        ok_n = jnp.logical_and(colk <= row, i < nb - 1)
        zeros = jnp.zeros((BLOCK, BLOCK), F32)
        bias = jnp.concatenate([jnp.zeros((BLOCK, n_ctx), F32), jnp.where(ok_p, 0.0, -jnp.inf), zeros,
                                jnp.where(ok_n, 0.0, -jnp.inf)], axis=1)
    keys = jnp.concatenate(k_parts, axis=0) if windowed else k_parts[0]
    vals = jnp.concatenate(v_parts, axis=0) if windowed else v_parts[0]
    s_all = _dot_nt(qz, keys)
    dens = []
    for h in range(A_HEADS):
        rows = slice(h * BLOCK, (h + 1) * BLOCK)
        s = s_all[rows]
        if windowed:
            s = s + bias
        sink = sink_ref[h] * math.log2(math.e)
        m = jnp.maximum(jnp.max(s, axis=-1, keepdims=True), sink)
        e = jnp.exp2(s - m)
        dens.append(jnp.sum(e, axis=-1, keepdims=True) + jnp.exp2(sink - m))
        e_ref[rows, :] = e.astype(BF16)
    o_all = _dot(e_ref[...], vals)
    for c in range(A_HEADS // 2):
        halves = []
        for h in (2 * c, 2 * c + 1):
            o = o_all[h * BLOCK:(h + 1) * BLOCK] / dens[h]
            if h % 2 != h // A_GROUP:
                o = pltpu.roll(o, HEAD_DIM, axis=1)
            halves.append(o)
        o_ref[:, c * LANES:(c + 1) * LANES] = jnp.where(lane < HEAD_DIM, halves[0], halves[1]).astype(o_ref.dtype)


def _attn_a(pa, sink, k_ctx, v_ctx, j, *, n_seq, seq_len):
    n = pa.shape[0]
    nb = seq_len // BLOCK
    kcol, vcol = A_Q // A_KV, A_Q // A_KV + 1
    smem = pl.BlockSpec(memory_space=pltpu.SMEM)
    q_spec = pl.BlockSpec((BLOCK, A_Q), lambda b, i: (b * nb + i, 0))
    if k_ctx is not None:
        n_ctx = k_ctx.shape[2]

        def blk(colblk, off):
            return pl.BlockSpec((BLOCK, A_KV), lambda b, i: (b * nb + jnp.clip(i + off, 0, nb - 1), colblk))

        in_specs = [smem, q_spec, blk(kcol, -1), blk(kcol, 0), blk(kcol, 1), blk(vcol, -1), blk(vcol, 0), blk(vcol, 1),
                    pl.BlockSpec((None, None, n_ctx, A_KV), lambda b, i: (b, j, 0, 0)),
                    pl.BlockSpec((None, None, n_ctx, A_KV), lambda b, i: (b, j, 0, 0))]
        args = [sink, pa, pa, pa, pa, pa, pa, pa, k_ctx, v_ctx]
    else:
        in_specs = [smem, q_spec,
                    pl.BlockSpec((seq_len, A_KV), lambda b, i: (b, kcol)),
                    pl.BlockSpec((seq_len, A_KV), lambda b, i: (b, vcol))]
        args = [sink, pa, pa, pa]
        n_ctx = seq_len
    n_keys = n_ctx + (3 * BLOCK if k_ctx is not None else 0)
    return pl.pallas_call(
        functools.partial(_attn_a_kernel, windowed=k_ctx is not None),
        grid=(n_seq, nb),
        in_specs=in_specs,
        out_specs=pl.BlockSpec((BLOCK, A_Q), lambda b, i: (b * nb + i, 0)),
        out_shape=jax.ShapeDtypeStruct((n, A_Q), BF16),
        scratch_shapes=[pltpu.VMEM((A_HEADS * BLOCK, n_keys), BF16)],
        compiler_params=_params("parallel", "parallel"),
        name="mixer_a_attention",
    )(*args)


def _attn_c_kernel(*refs, lam_init, n_lat, n_ctx):
    lam_ref, subln_ref, q_ref = refs[:3]
    rest = list(refs[3:])
    kx_ref = vx_ref = kl_ref = vl_ref = None
    if n_ctx:
        kx_ref, vx_ref = rest[:2]
        rest = rest[2:]
    if n_lat:
        kl_ref, vl_ref = rest[:2]
        rest = rest[2:]
    o_ref, vt_ref, s_ref, e_ref = rest
    KEY_TILE = s_ref.shape[1]
    tiles = ([(kx_ref, vx_ref, t) for t in range(0, n_ctx, KEY_TILE)]
             + [(kl_ref, vl_ref, t) for t in range(0, n_lat, KEY_TILE)])
    tq = q_ref.shape[0]
    cols = [slice(i * KEY_TILE, (i + 1) * KEY_TILE) for i in range(len(tiles))]

    @pl.when(pl.program_id(2) == 0)
    def _():
        for i, (_, v_ref, t) in enumerate(tiles):
            vt_ref[:C_VDIM, cols[i]] = v_ref[t:t + KEY_TILE, :].astype(F32).T.astype(BF16)
        vt_ref[C_VDIM:, :] = jnp.ones((vt_ref.shape[0] - C_VDIM, vt_ref.shape[1]), BF16)

    lam = (jnp.exp(jnp.sum(lam_ref[0:1, :] * lam_ref[1:2, :], axis=-1, keepdims=True))
           - jnp.exp(jnp.sum(lam_ref[2:3, :] * lam_ref[3:4, :], axis=-1, keepdims=True)) + lam_init)

    q = q_ref[...].astype(BF16)
    lane = lax.broadcasted_iota(jnp.int32, q.shape, 1)
    zero = jnp.zeros_like(q)
    qz = jnp.concatenate([jnp.where(lane < HEAD_DIM, q, zero), jnp.where(lane >= HEAD_DIM, q, zero)], axis=0)
    m = jnp.full((1, 2 * tq), -jnp.inf, F32)
    acc = jnp.zeros((vt_ref.shape[0], 2 * tq), F32)
    rb = 32

    def scores(i):
        k_ref, _, t = tiles[i]
        s = _dot_nt(k_ref[t:t + KEY_TILE, :].astype(BF16), qz)
        s_ref[i % 2] = s
        return jnp.max(s, axis=0, keepdims=True)

    top = scores(0)
    for i in range(len(tiles)):
        m_new = jnp.maximum(m, top)
        if i + 1 < len(tiles):
            top = scores(i + 1)
        corr = jnp.exp2(m - m_new)
        for r0 in range(0, KEY_TILE, rb):
            e_ref[i % 2, r0:r0 + rb, :] = jnp.exp2(s_ref[i % 2, r0:r0 + rb, :] - m_new).astype(BF16)
        acc = acc * corr + _dot(vt_ref[:, cols[i]], e_ref[i % 2])
        m = m_new
    l = acc[C_VDIM:C_VDIM + 1]
    acc = acc[:C_VDIM]
    o = acc[:, :tq] / l[:, :tq] - lam * (acc[:, tq:] / l[:, tq:])
    on = o * lax.rsqrt(jnp.mean(o * o, axis=0, keepdims=True) + SUBLN_EPS)
    o_ref[...] = (on.T * subln_ref[...] * (1.0 - lam_init)).astype(o_ref.dtype)


def _attn_c(pc, lam_vecs, subln, k_ctx, v_ctx, j, *, lam_init, n_seq, seq_len):
    n = pc.shape[0]
    tq = min(1024, seq_len)
    nq = seq_len // tq
    latent = k_ctx is not None
    in_specs = [pl.BlockSpec((4, HEAD_DIM), lambda b, h, i: (0, 0)),
                pl.BlockSpec((1, C_VDIM), lambda b, h, i: (0, 0)),
                pl.BlockSpec((tq, C_VDIM), lambda b, h, i: (b * nq + i, h))]
    args = [lam_vecs, subln, pc]
    own_k = pl.BlockSpec((seq_len, C_VDIM), lambda b, h, i: (b, C_HEADS + h))
    own_v = pl.BlockSpec((seq_len, C_VDIM), lambda b, h, i: (b, 2 * C_HEADS + h))
    if latent:
        n_ctx = k_ctx.shape[2]
        in_specs += [pl.BlockSpec((None, None, n_ctx, C_VDIM), lambda b, h, i: (b, j, 0, h))] * 2
        args += [k_ctx, v_ctx]
        n_lat = seq_len
    else:
        n_ctx, n_lat = seq_len, 0
    in_specs += [own_k, own_v]
    args += [pc, pc]
    key_tile = 512 if (n_ctx % 512 == 0 and n_lat % 512 == 0) else 256
    return pl.pallas_call(
        functools.partial(_attn_c_kernel, lam_init=lam_init, n_lat=n_lat, n_ctx=n_ctx),
        grid=(n_seq, C_HEADS, nq),
        in_specs=in_specs,
        out_specs=pl.BlockSpec((tq, C_VDIM), lambda b, h, i: (b * nq + i, h)),
        out_shape=jax.ShapeDtypeStruct((n, C_WIDTH), BF16),
        scratch_shapes=[pltpu.VMEM((C_VDIM + 16, n_ctx + n_lat), BF16),
                        pltpu.VMEM((2, key_tile, 2 * tq), F32),
                        pltpu.VMEM((2, key_tile, 2 * tq), BF16)],
        compiler_params=_params("parallel", "parallel", "arbitrary"),
        name="mixer_c_attention",
    )(*args)


def _pair_ones():
    idx = np.arange(LANES) // HEAD_DIM
    ones = (idx[:, None] == idx[None, :]).astype(np.float32)
    return jnp.asarray(np.concatenate([ones, ones], axis=0), dtype=BF16)


def _head_sum(x, ones2):
    out = []
    for p in range(PAIRS):
        hi, lo = _split(x[:, p * LANES:(p + 1) * LANES])
        out.append(_dot(jnp.concatenate([hi, lo], axis=1), ones2))
    return jnp.concatenate(out, axis=1)


def _rwkv_prep_kernel(pb_ref, hp_ref, hn_ref, mup_ref, mun_ref, w0_ref, w2_ref, a0_ref, a2_ref,
                      kk_ref, ka_ref, rk_ref, g2_ref, bd_ref,
                      r_out, v_out, kkn_out, g_out, bonus_out, lw_out, kd_out, ab_out, *, tiles_per_seq):
    i = pl.program_id(0)
    tm = pb_ref.shape[0]
    first = i % tiles_per_seq == 0
    last = i % tiles_per_seq == tiles_per_seq - 1
    pb = pb_ref[...]
    rowi = lax.broadcasted_iota(jnp.int32, (tm, 1), 0)
    prev_row = jnp.where(first, 0.0, hp_ref[7:8, :])
    next_row = jnp.where(last, 0.0, hn_ref[0:1, :])
    prev = jnp.where(rowi == 0, prev_row, pltpu.roll(pb, 1, axis=0))
    nxt = jnp.where(rowi == tm - 1, next_row, pltpu.roll(pb, tm - 1, axis=0))
    ps = pb + mup_ref[...] * (prev - pb) + mun_ref[...] * (nxt - pb)

    o = 3 * B_WIDTH
    r = ps[:, 0:B_WIDTH]
    k = ps[:, B_WIDTH:2 * B_WIDTH]
    v = ps[:, 2 * B_WIDTH:o]
    w_dn = (ps[:, o:o + DECAY_LORA], ps[:, o + DECAY_LORA:o + 2 * DECAY_LORA])
    o += 2 * DECAY_LORA
    a_dn = (ps[:, o:o + AAA_LORA], ps[:, o + AAA_LORA:o + 2 * AAA_LORA])
    o += 2 * AAA_LORA
    gd = ps[:, o:o + GATE_LORA]

    ones2 = bd_ref[...]
    kk = k * kk_ref[...]
    kk = kk * jnp.minimum(lax.rsqrt(_head_sum(kk * kk, ones2)), 1e12)
    r_out[...] = r
    v_out[...] = v
    kkn_out[...] = kk
    g_out[...] = _mm3(jax.nn.sigmoid(gd), g2_ref[...])
    bonus = None
    for d in range(2):
        wl = w0_ref[d:d + 1, :] + _mm3(jnp.tanh(w_dn[d]), w2_ref[d])
        w_log = -jax.nn.softplus(-wl) - 0.5
        lw_out[d] = -jnp.exp(w_log)
        a = jax.nn.sigmoid(a0_ref[d:d + 1, :] + _mm3(a_dn[d], a2_ref[d]))
        k_d = k * (1.0 + (a - 1.0) * ka_ref[...])
        kd_out[d] = k_d
        ab_out[d] = a * kk
        bo = _head_sum(r * k_d * rk_ref[...], ones2) * v
        bonus = bo if bonus is None else bonus + bo
    bonus_out[...] = bonus


def _rwkv_prep(pb, W, j, *, seq_len):
    n = pb.shape[0]
    tm = 256
    tps = seq_len // tm
    nblk8 = n // 8
    row = lambda a: a.reshape(1, -1)
    res = _resident
    in_specs = [pl.BlockSpec((tm, B_COLS), lambda i: (i, 0)),
                pl.BlockSpec((8, B_COLS), lambda i: (jnp.maximum(i * (tm // 8) - 1, 0), 0)),
                pl.BlockSpec((8, B_COLS), lambda i: (jnp.minimum((i + 1) * (tm // 8), nblk8 - 1), 0)),
                res((1, B_COLS)), res((1, B_COLS)),
                res((2, B_WIDTH)), res((2, DECAY_LORA, B_WIDTH)),
                res((2, B_WIDTH)), res((2, AAA_LORA, B_WIDTH)),
                res((1, B_WIDTH)), res((1, B_WIDTH)), res((1, B_WIDTH)),
                res((GATE_LORA, B_WIDTH)), res((2 * LANES, LANES))]
    one = pl.BlockSpec((tm, B_WIDTH), lambda i: (i, 0))
    two = pl.BlockSpec((2, tm, B_WIDTH), lambda i: (0, i, 0))
    s1 = jax.ShapeDtypeStruct((n, B_WIDTH), F32)
    s2 = jax.ShapeDtypeStruct((2, n, B_WIDTH), F32)
    return pl.pallas_call(
        functools.partial(_rwkv_prep_kernel, tiles_per_seq=tps),
        grid=(n // tm,),
        in_specs=in_specs,
        out_specs=[one, one, one, one, one, two, two, two],
        out_shape=[s1, s1, s1, s1, s1, s2, s2, s2],
        compiler_params=_params("parallel"),
        name="rwkv_prep",
    )(pb, pb, pb, row(W["b_mu_prev"][j]), row(W["b_mu_next"][j]), W["b_w0"][j], W["b_w2"][j],
      W["b_a0"][j], W["b_a2"][j], row(W["b_k_k"][j]), row(W["b_k_a"][j]), row(W["b_r_k"][j]),
      W["b_g2"][j], _pair_ones())


def _bd(x):
    lane = lax.broadcasted_iota(jnp.int32, x.shape, 1)
    return jnp.concatenate([jnp.where(lane < HEAD_DIM, x, 0.0), jnp.where(lane >= HEAD_DIM, x, 0.0)], axis=0)


def _pair_diag(full):
    lane = lax.broadcasted_iota(jnp.int32, (HEAD_DIM, LANES), 1)
    return jnp.where(lane < HEAD_DIM, full[:HEAD_DIM, :], full[HEAD_DIM:, :])


def _split(x):
    hi = x.astype(BF16)
    return hi, (x - hi.astype(F32)).astype(BF16)


def _lhs3(a):
    ah, al = _split(a)
    return jnp.concatenate([ah, al, ah], axis=1)


def _rhs3(b, bd, axis):
    bh, bl = _split(b)
    if bd:
        bh, bl = _bd(bh), _bd(bl)
    return jnp.concatenate([bh, bh, bl], axis=axis)


def _mm3(a, b, bd=False):
    bs = b if isinstance(b, (list, tuple)) else [b]
    rhs = [_rhs3(x, bd, 0) for x in bs]
    return _dot(_lhs3(a), rhs[0] if len(rhs) == 1 else jnp.concatenate(rhs, axis=1))


def _mm3_nt(a, bs):
    rhs = [_rhs3(x, True, 1) for x in bs]
    return _dot_nt(_lhs3(a), rhs[0] if len(rhs) == 1 else jnp.concatenate(rhs, axis=0))


def _mm3_tn(a, b):
    (ah, al), (bh, bl) = _split(a), _split(b)
    return _dot_tn(jnp.concatenate([ah, al, ah], axis=0), jnp.concatenate([bh, bh, bl], axis=0))


STEP_CHUNKS = 2


def _rwkv_chunk_kernel(r_ref, v_ref, kk_ref, lw_ref, kd_ref, ab_ref, m_out, n_out, qt_out, pv_out):
    d = pl.program_id(0)
    row = lax.broadcasted_iota(jnp.int32, (CHUNK, LANES), 0)
    col = lax.broadcasted_iota(jnp.int32, (CHUNK, LANES), 1) % HEAD_DIM
    ahead = jnp.where(d == 0, row - col, col - row)
    strict = ahead > 0
    incl = ahead >= 0
    eye = (col == row).astype(F32)
    tri = jnp.where(incl[:, :CHUNK], 1.0, 0.0).astype(BF16)
    H = CHUNK
    units = [(g, p) for g in range(STEP_CHUNKS) for p in range(PAIRS)]
    at = [(slice(g * CHUNK, (g + 1) * CHUNK), slice(p * LANES, (p + 1) * LANES)) for g, p in units]
    st = [(g, slice(p * HEAD_DIM, (p + 1) * HEAD_DIM)) for g, p in units]

    lw = [lw_ref[t, ls] for t, ls in at]
    cum = []
    for x in lw:
        hi = x.astype(BF16)
        r1 = x - hi.astype(F32)
        mid = r1.astype(BF16)
        lo = (r1 - mid.astype(F32)).astype(BF16)
        y = _dot(tri, jnp.concatenate([hi, mid, lo], axis=1))
        cum.append((y[:, :LANES] + y[:, LANES:2 * LANES]) + y[:, 2 * LANES:])
    w_inv = [jnp.exp(-c) for c in cum]
    w_all = [jnp.exp(jnp.sum(x, axis=0, keepdims=True)) for x in lw]
    alpha = [jnp.exp(c - x) * kk_ref[t, ls] for c, x, (t, ls) in zip(cum, lw, at)]
    beta = [ab_ref[t, ls] * w for w, (t, ls) in zip(w_inv, at)]
    kappa = [kd_ref[t, ls] * w for w, (t, ls) in zip(w_inv, at)]
    rho = [jnp.exp(c) * r_ref[t, ls] for c, (t, ls) in zip(cum, at)]

    cross = [_mm3_nt(jnp.concatenate([a, r], axis=0), [b, k]) for a, r, b, k in zip(alpha, rho, beta, kappa)]
    l_ab = [jnp.where(strict, y[:H, :LANES], 0.0) for y in cross]
    l_rb = [jnp.where(incl, y[H:, :LANES], 0.0) for y in cross]
    l_ak = [jnp.where(strict, y[:H, LANES:], 0.0) for y in cross]
    l_rk = [jnp.where(incl, y[H:, LANES:], 0.0) for y in cross]

    pw = [-x for x in l_ab]
    t_inv = [eye + x for x in pw]
    pw = [_mm3(x, x, bd=True) for x in pw]
    for _ in range(int(math.log2(CHUNK)) - 2):
        both = [_mm3(jnp.concatenate([t, x], axis=0), x, bd=True) for t, x in zip(t_inv, pw)]
        t_inv = [t + y[:H] for t, y in zip(t_inv, both)]
        pw = [y[H:] for y in both]
    t_inv = [t + _mm3(t, x, bd=True) for t, x in zip(t_inv, pw)]

    xs = [_mm3(t, [a, x], bd=True) for t, a, x in zip(t_inv, alpha, l_ak)]
    xa = [x[:, :LANES] for x in xs]
    xk = [x[:, LANES:] for x in xs]
    lx = [_mm3(lb, [a, k], bd=True) for lb, a, k in zip(l_rb, xa, xk)]
    for u, (t, ls) in enumerate(at):
        qt_out[t, ls] = (rho[u] - lx[u][:, :LANES]).astype(qt_out.dtype)
    for u, (t, ls) in enumerate(at):
        pm = l_rk[u] - lx[u][:, LANES:]
        pv_out[t, ls] = _dot(pm.astype(BF16), _bd(v_ref[t, ls].astype(BF16)))
    g = [_mm3_tn(x, b) for x, b in zip(xs, beta)]
    for u, (c, rs) in enumerate(st):
        m_out[c, rs, :] = (eye - _pair_diag(g[u][:LANES])) * w_all[u]
    z = [k - _pair_diag(y[LANES:]) for k, y in zip(kappa, g)]
    for u, ((t, ls), (c, rs)) in enumerate(zip(at, st)):
        n_out[c, rs, :] = _pair_diag(_mm3_tn(v_ref[t, ls], z[u])) * w_all[u]


def _rwkv_chunks(r, v, kk, lw, kd, ab, *, n_seq, seq_len):
    n = r.shape[0]
    nc = seq_len // CHUNK
    ng = nc // STEP_CHUNKS
    tm = CHUNK * STEP_CHUNKS
    one = pl.BlockSpec((tm, B_WIDTH), lambda d, b, c: (b * ng + c, 0))
    two = pl.BlockSpec((None, tm, B_WIDTH), lambda d, b, c: (d, b * ng + c, 0))
    st = pl.BlockSpec((None, None, STEP_CHUNKS, PAIRS * HEAD_DIM, LANES), lambda d, b, c: (d, b, c, 0, 0))
    st_shape = jax.ShapeDtypeStruct((2, n_seq, nc, PAIRS * HEAD_DIM, LANES), F32)
    return pl.pallas_call(
        _rwkv_chunk_kernel,
        grid=(2, n_seq, ng),
        in_specs=[one, one, one, two, two, two],
        out_specs=[st, st, two, two],
        out_shape=[st_shape, st_shape, jax.ShapeDtypeStruct((2, n, B_WIDTH), BF16),
                   jax.ShapeDtypeStruct((2, n, B_WIDTH), F32)],
        compiler_params=_params("parallel", "parallel", "parallel"),
        name="rwkv_chunk_operators",
    )(r, v, kk, lw, kd, ab)


def _rwkv_scan_kernel(s0_ref, m_ref, n_ref, start_out, fin_out, s_scr):
    c = pl.program_id(2)
    SCAN_SEQS = s_scr.shape[0]

    @pl.when(c == 0)
    def _():
        s_scr[...] = s0_ref[...]

    for b in range(SCAN_SEQS):
        for p in range(PAIRS):
            rs = slice(p * HEAD_DIM, (p + 1) * HEAD_DIM)
            s = s_scr[b, rs, :]
            start_out[b, rs, :] = s
            s_scr[b, rs, :] = _mm3(s, m_ref[b, rs, :], bd=True) + n_ref[b, rs, :]

    @pl.when(c == pl.num_programs(2) - 1)
    def _():
        fin_out[...] = s_scr[...]


def _rwkv_scan(s0, m, nn):
    _, n_seq, nc, rows, _ = m.shape
    SCAN_SEQS = min(8, n_seq)
    chunk_of = lambda d, c: c + d * (nc - 1 - 2 * c)
    blk = pl.BlockSpec((None, SCAN_SEQS, None, rows, LANES), lambda d, g, c: (d, g, chunk_of(d, c), 0, 0))
    ends = pl.BlockSpec((None, SCAN_SEQS, rows, LANES), lambda d, g, c: (d, g, 0, 0))
    return pl.pallas_call(
        _rwkv_scan_kernel,
        grid=(2, n_seq // SCAN_SEQS, nc),
        in_specs=[ends, blk, blk],
        out_specs=[blk, ends],
        out_shape=[jax.ShapeDtypeStruct(m.shape, F32), jax.ShapeDtypeStruct((2, n_seq, rows, LANES), F32)],
        scratch_shapes=[pltpu.VMEM((SCAN_SEQS, rows, LANES), F32)],
        compiler_params=_params("parallel", "parallel", "arbitrary"),
        name="rwkv_state_scan",
    )(s0, m, nn)


OUT_CHUNKS = 4


def _rwkv_out_kernel(qf_ref, qb_ref, pf_ref, pb_ref, sf_ref, sb_ref, bonus_ref, g_ref, gw_ref, gb_ref, ones_ref,
                     o_ref, y_scr):
    for c in range(OUT_CHUNKS):
        tok = slice(c * CHUNK, (c + 1) * CHUNK)
        for p in range(PAIRS):
            ls = slice(p * LANES, (p + 1) * LANES)
            rs = slice(p * HEAD_DIM, (p + 1) * HEAD_DIM)
            y = pf_ref[tok, ls] + pb_ref[tok, ls]
            y = y + _dot_nt(qf_ref[tok, ls], _bd(sf_ref[c, rs, :].astype(BF16)))
            y = y + _dot_nt(qb_ref[tok, ls], _bd(sb_ref[c, rs, :].astype(BF16)))
            y_scr[tok, ls] = y
    y = y_scr[...]
    yc = y - _head_sum(y, ones_ref[...]) * (1.0 / HEAD_DIM)
    var = _head_sum(yc * yc, ones_ref[...]) * (1.0 / HEAD_DIM)
    yn = yc * lax.rsqrt(var + GN_EPS) * gw_ref[...] + gb_ref[...] + bonus_ref[...]
    o_ref[...] = (yn * g_ref[...]).astype(o_ref.dtype)


def _rwkv_out(qt, pv, starts, bonus, g, gn_w, gn_b, *, n_seq, seq_len):
    n = bonus.shape[0]
    ng = seq_len // (CHUNK * OUT_CHUNKS)
    tm = CHUNK * OUT_CHUNKS
    rows = PAIRS * HEAD_DIM
    tok = lambda d: pl.BlockSpec((None, tm, B_WIDTH), lambda b, c: (d, b * ng + c, 0))
    st = lambda d: pl.BlockSpec((None, None, OUT_CHUNKS, rows, LANES), lambda b, c: (d, b, c, 0, 0))
    one = pl.BlockSpec((tm, B_WIDTH), lambda b, c: (b * ng + c, 0))
    vec = pl.BlockSpec((1, B_WIDTH), lambda b, c: (0, 0))
    return pl.pallas_call(
        _rwkv_out_kernel,
        grid=(n_seq, ng),
        in_specs=[tok(0), tok(1), tok(0), tok(1), st(0), st(1), one, one, vec, vec,
                  pl.BlockSpec((2 * LANES, LANES), lambda b, c: (0, 0))],
        out_specs=one,
        out_shape=jax.ShapeDtypeStruct((n, B_WIDTH), BF16),
        scratch_shapes=[pltpu.VMEM((tm, B_WIDTH), F32)],
        compiler_params=_params("parallel", "parallel"),
        name="rwkv_output",
    )(qt, qt, pv, pv, starts, starts, bonus, g, gn_w.reshape(1, -1), gn_b.reshape(1, -1), _pair_ones())


def _to_pairs(s):
    lead = s.shape[:-3]
    s = s.reshape(*lead, PAIRS, 2, HEAD_DIM, HEAD_DIM)
    s = jnp.swapaxes(s, -3, -2)
    return s.reshape(*lead, PAIRS * HEAD_DIM, LANES)


def _from_pairs(s):
    lead = s.shape[:-2]
    s = s.reshape(*lead, PAIRS, HEAD_DIM, 2, HEAD_DIM)
    s = jnp.swapaxes(s, -3, -2)
    return s.reshape(*lead, B_HEADS, HEAD_DIM, HEAD_DIM)


def _rwkv_mixer(pb, s0, j, W, *, n_seq, seq_len):
    r, v, kk, g, bonus, lw, kd, ab = _rwkv_prep(pb, W, j, seq_len=seq_len)
    m, nn, qt, pv = _rwkv_chunks(r, v, kk, lw, kd, ab, n_seq=n_seq, seq_len=seq_len)
    starts, fin = _rwkv_scan(s0, m, nn)
    ob = _rwkv_out(qt, pv, starts, bonus, g, W["b_gn_w"][j], W["b_gn_b"][j], n_seq=n_seq, seq_len=seq_len)
    return ob, fin


def _forward(x, mod, caches, W, Wb, *, row0):
    n_seq, seq_len, _ = x.shape
    latent = caches is not None
    x = x.reshape(n_seq * seq_len, D_MODEL)
    rope = _rope_tables(seq_len) if latent else None
    kw = dict(seq_len=seq_len, row0=row0)
    norm = lambda l, s: W["norms"][l, s].reshape(1, D_MODEL)
    gf = W["final_norm"].reshape(1, D_MODEL)
    ak, av, bs, ck, cv = [], [], [], [], []
    for l in range(DEPTH):
        j = l // 2
        x = _ffn(x, mod[l], norm(l, 0), Wb["ffn_w1"][l, 0], Wb["ffn_w3"][l, 0], Wb["ffn_w2"][l, 0], gf, sub=0, **kw)
        if l % 2 == 0:
            pa, pb = _proj(x, mod[l], norm(l, 1), Wb["w_in_ab"][j], rope, sub=1, splits=(A_COLS, B_COLS),
                           dtypes=(BF16 if latent else F32, F32), n_rope=(A_Q + A_KV) if latent else 0,
                           n_q=A_Q, **kw)
            if latent:
                k_ctx = caches["a_k"].reshape(*caches["a_k"].shape[:3], A_KV)
                v_ctx = caches["a_v"].reshape(*caches["a_v"].shape[:3], A_KV)
                s0 = _to_pairs(jnp.swapaxes(caches["b_s"][:, j], 0, 1))
            else:
                k_ctx = v_ctx = None
                s0 = jnp.zeros((2, n_seq, PAIRS * HEAD_DIM, LANES), F32)
                ak.append(pa[:, A_Q:A_Q + A_KV].reshape(n_seq, seq_len, A_KV_HEADS, HEAD_DIM))
                av.append(pa[:, A_Q + A_KV:].reshape(n_seq, seq_len, A_KV_HEADS, HEAD_DIM))
            oa = _attn_a(pa, W["a_sink"][j], k_ctx, v_ctx, j, n_seq=n_seq, seq_len=seq_len)
            ob, fin = _rwkv_mixer(pb, s0, j, W, n_seq=n_seq, seq_len=seq_len)
            if not latent:
                bs.append(jnp.swapaxes(_from_pairs(fin), 0, 1))
            w_out = Wb["w_out_ab"][j]
            mix = dict(mix_acts=(oa, ob), mix_ws=(w_out[:A_Q], w_out[A_Q:]))
        else:
            (pc,) = _proj(x, mod[l], norm(l, 1), Wb["w_in_c"][j], rope, sub=1, splits=(3 * C_WIDTH,),
                          dtypes=(BF16 if latent else F32,), n_rope=2 * C_WIDTH if latent else 0,
                          n_q=C_WIDTH, **kw)
            if latent:
                k_ctx = caches["c_k"].reshape(*caches["c_k"].shape[:3], C_WIDTH)
                v_ctx = caches["c_v"].reshape(*caches["c_v"].shape[:3], C_WIDTH)
            else:
                k_ctx = v_ctx = None
                ck.append(pc[:, C_WIDTH:2 * C_WIDTH].reshape(n_seq, seq_len, C_HEADS, C_VDIM))
                cv.append(pc[:, 2 * C_WIDTH:].reshape(n_seq, seq_len, C_HEADS, C_VDIM))
            lam_vecs = jnp.stack([W["c_lq1"][j], W["c_lk1"][j], W["c_lq2"][j], W["c_lk2"][j]])
            oc = _attn_c(pc, lam_vecs, W["c_subln"][j].reshape(1, C_VDIM), k_ctx, v_ctx, j,
                         lam_init=0.8 - 0.6 * math.exp(-0.3 * l), n_seq=n_seq, seq_len=seq_len)
            mix = dict(mix_acts=(oc,), mix_ws=(Wb["w_out_c"][j],))
        x = _ffn(x, mod[l], norm(l, 2), Wb["ffn_w1"][l, 1], Wb["ffn_w3"][l, 1], Wb["ffn_w2"][l, 1], gf,
                 sub=2, final=(l == DEPTH - 1), **mix, **kw)
    return x.reshape(n_seq, seq_len, D_MODEL), (ak, av, bs, ck, cv)


def kernel(x_prompt, x_sample, c, c_ctx, cache_a_k, cache_a_v, state_b_wkv, cache_c_k, cache_c_v, norms, mod_w, mod_b, ffn_w1, ffn_w3, ffn_w2, w_in_ab, w_out_ab, a_sink, b_mu_prev, b_mu_next, b_w0, b_w2, b_a0, b_a2, b_k_k, b_k_a, b_r_k, b_g2, b_gn_w, b_gn_b, w_in_c, w_out_c, c_lq1, c_lk1, c_lq2, c_lk2, c_subln, final_norm):
    W = dict(norms=norms, a_sink=a_sink, b_mu_prev=b_mu_prev, b_mu_next=b_mu_next,
             b_w0=b_w0, b_w2=b_w2, b_a0=b_a0, b_a2=b_a2, b_k_k=b_k_k, b_k_a=b_k_a, b_r_k=b_r_k,
             b_g2=b_g2, b_gn_w=b_gn_w, b_gn_b=b_gn_b,
             c_lq1=c_lq1, c_lk1=c_lk1, c_lq2=c_lq2, c_lk2=c_lk2, c_subln=c_subln, final_norm=final_norm)
    Wb = dict(ffn_w1=ffn_w1.astype(BF16), ffn_w3=ffn_w3.astype(BF16), ffn_w2=ffn_w2.astype(BF16),
              w_in_ab=w_in_ab.astype(BF16), w_out_ab=w_out_ab.astype(BF16),
              w_in_c=w_in_c.astype(BF16), w_out_c=w_out_c.astype(BF16))
    n_lat = c.shape[0]
    assert 1 + n_lat <= MOD_ROWS
    cond = jnp.zeros((MOD_ROWS, D_MODEL), F32).at[0].set(c_ctx).at[1:1 + n_lat].set(c)
    mod = _modulation(cond, mod_w, mod_b)
    y_prompt, (ak, av, bs, ck, cv) = _forward(x_prompt, mod, None, W, Wb, row0=0)
    caches = dict(a_k=cache_a_k, a_v=cache_a_v, b_s=state_b_wkv, c_k=cache_c_k, c_v=cache_c_v)
    y_sample, _ = _forward(x_sample, mod, caches, W, Wb, row0=1)
    return (y_prompt, y_sample, jnp.stack(ak, axis=1), jnp.stack(av, axis=1), jnp.stack(bs, axis=1),
            jnp.stack(ck, axis=1), jnp.stack(cv, axis=1))
```

```python
import functools
import math

import numpy as np
import jax
import jax.numpy as jnp
from jax import lax
from jax.experimental import pallas as pl
from jax.experimental.pallas import tpu as pltpu

F32 = jnp.float32
BF16 = jnp.bfloat16
HIGHEST = lax.Precision.HIGHEST

D_MODEL = 1024
DEPTH = 4
GRID_W = 64
HEAD_DIM = 64
BLOCK = 128
A_HEADS = 8
A_KV_HEADS = 2
A_GROUP = A_HEADS // A_KV_HEADS
A_Q = A_HEADS * HEAD_DIM
A_KV = A_KV_HEADS * HEAD_DIM
A_COLS = A_Q + 2 * A_KV
B_HEADS = 8
B_WIDTH = B_HEADS * HEAD_DIM
DECAY_LORA = 64
AAA_LORA = 64
GATE_LORA = 128
B_COLS = 3 * B_WIDTH + 2 * DECAY_LORA + 2 * AAA_LORA + GATE_LORA
C_HEADS = 8
C_VDIM = 2 * HEAD_DIM
C_WIDTH = C_HEADS * C_VDIM
D_FF = 2816
N_MOD = 9
ROPE_THETA = 10000.0
NORM_EPS = 1e-6
GN_EPS = 64e-5
SUBLN_EPS = 1e-5

LANES = 128
MOD_ROWS = 16
CHUNK = 64
PAIRS = B_HEADS // 2
VMEM_LIMIT = 56 * 2 ** 20


def _params(*sem):
    return pltpu.CompilerParams(dimension_semantics=sem, vmem_limit_bytes=VMEM_LIMIT)


def _dot(a, b, precision=None):
    return jnp.dot(a, b, preferred_element_type=F32, precision=precision)


def _dot_nt(a, b, precision=None):
    return lax.dot_general(a, b, (((1,), (1,)), ((), ())), preferred_element_type=F32, precision=precision)


def _dot_tn(a, b, precision=None):
    return lax.dot_general(a, b, (((0,), (0,)), ((), ())), preferred_element_type=F32, precision=precision)


def _silu(x):
    return x * jax.nn.sigmoid(x)


def _modnorm(x, g, scale, shift):
    y = x * lax.rsqrt(jnp.mean(x * x, axis=-1, keepdims=True) + NORM_EPS)
    return (y * g) * (1.0 + scale) + shift


def _mod_kernel(c_ref, w_ref, b_ref, o_ref):
    o_ref[...] = _dot(_silu(c_ref[...]), w_ref[...], HIGHEST) + b_ref[...]


def _modulation(cond, mod_w, mod_b):
    n_col = N_MOD * D_MODEL
    tn = D_MODEL
    out = pl.pallas_call(
        _mod_kernel,
        grid=(DEPTH, n_col // tn),
        in_specs=[pl.BlockSpec((MOD_ROWS, D_MODEL), lambda l, j: (0, 0)),
                  pl.BlockSpec((None, D_MODEL, tn), lambda l, j: (l, 0, j)),
                  pl.BlockSpec((None, 1, tn), lambda l, j: (l, 0, j))],
        out_specs=pl.BlockSpec((None, MOD_ROWS, tn), lambda l, j: (l, 0, j)),
        out_shape=jax.ShapeDtypeStruct((DEPTH, MOD_ROWS, n_col), F32),
        compiler_params=_params("parallel", "parallel"),
        name="modulation",
    )(cond, mod_w, mod_b.reshape(DEPTH, 1, n_col))
    return out.reshape(DEPTH, MOD_ROWS, N_MOD, D_MODEL)


def _mod_spec(tm, seq_len, row0):
    if row0 == 0:
        return pl.BlockSpec((None, N_MOD, D_MODEL), lambda i: (0, 0, 0))
    return pl.BlockSpec((None, N_MOD, D_MODEL), lambda i: (row0 + (i * tm) // seq_len, 0, 0))


def _resident(shape):
    nd = len(shape)
    return pl.BlockSpec(shape, lambda i: (0,) * nd, pipeline_mode=pl.Buffered(1))


FF_TILE = 256


def _ffn_kernel(*refs, sub, final, n_mix):
    x_ref, mod_ref, g_ref, w1_ref, w3_ref, w2_ref, gf_ref = refs[:7]
    mix_refs = refs[7:7 + 2 * n_mix]
    o_ref, act_ref = refs[7 + 2 * n_mix:]
    x = x_ref[...]
    if n_mix:
        y = _dot(mix_refs[0][...], mix_refs[n_mix][...])
        for a_ref, w_ref in zip(mix_refs[1:n_mix], mix_refs[n_mix + 1:]):
            y = y + _dot(a_ref[...], w_ref[...])
        x = x + mod_ref[5:6, :] * y
    h = _modnorm(x, g_ref[...], mod_ref[3 * sub + 1:3 * sub + 2, :], mod_ref[3 * sub:3 * sub + 1, :]).astype(BF16)
    for j in range(D_FF // FF_TILE):
        cols = slice(j * FF_TILE, (j + 1) * FF_TILE)
        a = _dot(h, w1_ref[:, cols])
        b = _dot(h, w3_ref[:, cols])
        act_ref[:, cols] = (_silu(a) * b).astype(BF16)
    y = x + 0.5 * mod_ref[3 * sub + 2:3 * sub + 3, :] * _dot(act_ref[...], w2_ref[...])
    if final:
        y = y * lax.rsqrt(jnp.mean(y * y, axis=-1, keepdims=True) + NORM_EPS) * gf_ref[...]
    o_ref[...] = y


def _ffn(x, mod_l, g, w1, w3, w2, gf, *, sub, seq_len, row0, final=False, mix_acts=(), mix_ws=()):
    n = x.shape[0]
    tm = min(1024, seq_len)
    return pl.pallas_call(
        functools.partial(_ffn_kernel, sub=sub, final=final, n_mix=len(mix_acts)),
        grid=(n // tm,),
        in_specs=([pl.BlockSpec((tm, D_MODEL), lambda i: (i, 0)),
                   _mod_spec(tm, seq_len, row0),
                   _resident((1, D_MODEL)),
                   _resident((D_MODEL, D_FF)),
                   _resident((D_MODEL, D_FF)),
                   _resident((D_FF, D_MODEL)),
                   _resident((1, D_MODEL))]
                  + [pl.BlockSpec((tm, a.shape[1]), lambda i: (i, 0)) for a in mix_acts]
                  + [_resident(w.shape) for w in mix_ws]),
        out_specs=pl.BlockSpec((tm, D_MODEL), lambda i: (i, 0)),
        out_shape=jax.ShapeDtypeStruct((n, D_MODEL), F32),
        scratch_shapes=[pltpu.VMEM((tm, D_FF), BF16)],
        compiler_params=_params("parallel"),
        name="ffn",
    )(x, mod_l, g, w1, w3, w2, gf, *mix_acts, *mix_ws)


def _swap16(x):
    lane = lax.broadcasted_iota(jnp.int32, x.shape, 1)
    return jnp.where(lane % 32 < 16, pltpu.roll(x, LANES - 16, axis=1), pltpu.roll(x, 16, axis=1))


Q_SCALE = HEAD_DIM ** -0.5 * math.log2(math.e)


def _proj_kernel(*refs, sub, splits, n_rope, n_q):
    if n_rope:
        x_ref, mod_ref, g_ref, w_ref, cos_ref, sin_ref = refs[:6]
        o_refs = refs[6:]
    else:
        x_ref, mod_ref, g_ref, w_ref = refs[:4]
        o_refs = refs[4:]
    h = _modnorm(x_ref[...], g_ref[...], mod_ref[3 * sub + 1:3 * sub + 2, :], mod_ref[3 * sub:3 * sub + 1, :]).astype(BF16)
    start = 0
    for o_ref, width in zip(o_refs, splits):
        for c in range(0, width, 256):
            wd = min(256, width - c)
            y = _dot(h, w_ref[:, start + c:start + c + wd])
            for p in range(0, wd, LANES):
                yp = y[:, p:p + LANES]
                if start + c + p < n_q:
                    yp = yp * Q_SCALE
                if start + c + p < n_rope:
                    yp = yp * cos_ref[...] + _swap16(yp) * sin_ref[...]
                o_ref[:, c + p:c + p + LANES] = yp.astype(o_ref.dtype)
        start += width


def _proj(x, mod_l, g, w, rope, *, sub, splits, dtypes, n_rope, n_q, seq_len, row0):
    n = x.shape[0]
    tm = min(1024, seq_len)
    n_col = w.shape[1]
    in_specs = [pl.BlockSpec((tm, D_MODEL), lambda i: (i, 0)),
                _mod_spec(tm, seq_len, row0),
                _resident((1, D_MODEL)),
                _resident((D_MODEL, n_col))]
    args = [x, mod_l, g, w]
    if n_rope:
        tiles = seq_len // tm
        in_specs += [pl.BlockSpec((tm, LANES), lambda i: (i % tiles, 0))] * 2
        args += list(rope)
    return pl.pallas_call(
        functools.partial(_proj_kernel, sub=sub, splits=splits, n_rope=n_rope, n_q=n_q),
        grid=(n // tm,),
        in_specs=in_specs,
        out_specs=[pl.BlockSpec((tm, wd), lambda i: (i, 0)) for wd in splits],
        out_shape=[jax.ShapeDtypeStruct((n, wd), dt) for wd, dt in zip(splits, dtypes)],
        compiler_params=_params("parallel"),
        name="mixer_in_proj",
    )(*args)


def _rope_tables(n_tok):
    t = jnp.arange(n_tok, dtype=jnp.int32)
    row = (t // GRID_W).astype(F32)
    col = (t % GRID_W).astype(F32)
    n_freq = HEAD_DIM // 4
    inv = jnp.power(ROPE_THETA, -jnp.arange(n_freq, dtype=F32) / n_freq)
    ar, ac = row[:, None] * inv, col[:, None] * inv
    cos = jnp.concatenate([jnp.cos(ar), jnp.cos(ar), jnp.cos(ac), jnp.cos(ac)], axis=1)
    sin = jnp.concatenate([-jnp.sin(ar), jnp.sin(ar), -jnp.sin(ac), jnp.sin(ac)], axis=1)
    return jnp.tile(cos, (1, 2)), jnp.tile(sin, (1, 2))


A_SPLIT = 4


def _attn_a_kernel(*refs, windowed):
    if windowed:
        sink_ref, q_ref, kp_ref, kc_ref, kn_ref, vp_ref, vc_ref, vn_ref, kx_ref, vx_ref, o_ref, e_ref = refs
    else:
        sink_ref, q_ref, kx_ref, vx_ref, o_ref, e_ref = refs
    i = pl.program_id(1)
    nb = pl.num_programs(1)
    n_ctx = kx_ref.shape[0]
    lane = lax.broadcasted_iota(jnp.int32, (BLOCK, LANES), 1)

    q32 = q_ref[...].astype(F32)
    blocks = []
    for h in range(A_HEADS):
        g = h // A_GROUP
        col = q32[:, (h // 2) * LANES:(h // 2 + 1) * LANES]
        if h % 2 != g:
            col = pltpu.roll(col, HEAD_DIM, axis=1)
        blocks.append(jnp.where(lane // HEAD_DIM == g, col, 0.0).astype(BF16))
    qz = jnp.concatenate(blocks, axis=0)

    k_parts = [kx_ref[...].astype(BF16)]
    v_parts = [vx_ref[...].astype(BF16)]
    if windowed:
        k_parts += [r[...].astype(BF16) for r in (kp_ref, kc_ref, kn_ref)]
        v_parts += [r[...].astype(BF16) for r in (vp_ref, vc_ref, vn_ref)]
        row = lax.broadcasted_iota(jnp.int32, (BLOCK, BLOCK), 0)
        colk = lax.broadcasted_iota(jnp.int32, (BLOCK, BLOCK), 1)
        ok_p = jnp.logical_and(colk >= row, i > 0)
        ok_n = jnp.logical_and(colk <= row, i < nb - 1)
        zeros = jnp.zeros((BLOCK, BLOCK), F32)
        bias = jnp.concatenate([jnp.zeros((BLOCK, n_ctx), F32), jnp.where(ok_p, 0.0, -jnp.inf), zeros,
                                jnp.where(ok_n, 0.0, -jnp.inf)], axis=1)
    keys = jnp.concatenate(k_parts, axis=0) if windowed else k_parts[0]
    vals = jnp.concatenate(v_parts, axis=0) if windowed else v_parts[0]
    hpg = A_HEADS // A_SPLIT
    rows_g = hpg * BLOCK
    s_grp = [_dot_nt(qz[g * rows_g:(g + 1) * rows_g], keys) for g in range(A_SPLIT)]
    dens, o_grp = [], []
    for g in range(A_SPLIT):
        for hh in range(hpg):
            h = g * hpg + hh
            s = s_grp[g][hh * BLOCK:(hh + 1) * BLOCK]
            if windowed:
                s = s + bias
            sink = sink_ref[h] * math.log2(math.e)
            m = jnp.maximum(jnp.max(s, axis=-1, keepdims=True), sink)
            e = jnp.exp2(s - m)
            dens.append(jnp.sum(e, axis=-1, keepdims=True) + jnp.exp2(sink - m))
            e_ref[h * BLOCK:(h + 1) * BLOCK, :] = e.astype(BF16)
        o_grp.append(_dot(e_ref[g * rows_g:(g + 1) * rows_g, :], vals))
    for c in range(A_HEADS // 2):
        halves = []
        for h in (2 * c, 2 * c + 1):
            o = o_grp[h // hpg][(h % hpg) * BLOCK:(h % hpg + 1) * BLOCK] / dens[h]
            if h % 2 != h // A_GROUP:
                o = pltpu.roll(o, HEAD_DIM, axis=1)
            halves.append(o)
        o_ref[:, c * LANES:(c + 1) * LANES] = jnp.where(lane < HEAD_DIM, halves[0], halves[1]).astype(o_ref.dtype)


def _attn_a(pa, sink, k_ctx, v_ctx, j, *, n_seq, seq_len):
    n = pa.shape[0]
    nb = seq_len // BLOCK
    kcol, vcol = A_Q // A_KV, A_Q // A_KV + 1
    smem = pl.BlockSpec(memory_space=pltpu.SMEM)
    q_spec = pl.BlockSpec((BLOCK, A_Q), lambda b, i: (b * nb + i, 0))
    if k_ctx is not None:
        n_ctx = k_ctx.shape[2]

        def blk(colblk, off):
            return pl.BlockSpec((BLOCK, A_KV), lambda b, i: (b * nb + jnp.clip(i + off, 0, nb - 1), colblk))

        in_specs = [smem, q_spec, blk(kcol, -1), blk(kcol, 0), blk(kcol, 1), blk(vcol, -1), blk(vcol, 0), blk(vcol, 1),
                    pl.BlockSpec((None, None, n_ctx, A_KV), lambda b, i: (b, j, 0, 0)),
                    pl.BlockSpec((None, None, n_ctx, A_KV), lambda b, i: (b, j, 0, 0))]
        args = [sink, pa, pa, pa, pa, pa, pa, pa, k_ctx, v_ctx]
    else:
        in_specs = [smem, q_spec,
                    pl.BlockSpec((seq_len, A_KV), lambda b, i: (b, kcol)),
                    pl.BlockSpec((seq_len, A_KV), lambda b, i: (b, vcol))]
        args = [sink, pa, pa, pa]
        n_ctx = seq_len
    n_keys = n_ctx + (3 * BLOCK if k_ctx is not None else 0)
    return pl.pallas_call(
        functools.partial(_attn_a_kernel, windowed=k_ctx is not None),
        grid=(n_seq, nb),
        in_specs=in_specs,
        out_specs=pl.BlockSpec((BLOCK, A_Q), lambda b, i: (b * nb + i, 0)),
        out_shape=jax.ShapeDtypeStruct((n, A_Q), BF16),
        scratch_shapes=[pltpu.VMEM((A_HEADS * BLOCK, n_keys), BF16)],
        compiler_params=_params("parallel", "parallel"),
        name="mixer_a_attention",
    )(*args)


def _attn_c_kernel(*refs, lam_init, n_lat, n_ctx):
    lam_ref, subln_ref, q_ref = refs[:3]
    rest = list(refs[3:])
    kx_ref = vx_ref = kl_ref = vl_ref = None
    if n_ctx:
        kx_ref, vx_ref = rest[:2]
        rest = rest[2:]
    if n_lat:
        kl_ref, vl_ref = rest[:2]
        rest = rest[2:]
    o_ref, vt_ref, s_ref, e_ref = rest
    KEY_TILE = s_ref.shape[1]
    tiles = ([(kx_ref, vx_ref, t) for t in range(0, n_ctx, KEY_TILE)]
             + [(kl_ref, vl_ref, t) for t in range(0, n_lat, KEY_TILE)])
    tq = q_ref.shape[0]
    cols = [slice(i * KEY_TILE, (i + 1) * KEY_TILE) for i in range(len(tiles))]

    @pl.when(pl.program_id(2) == 0)
    def _():
        for i, (_, v_ref, t) in enumerate(tiles):
            vt_ref[:C_VDIM, cols[i]] = v_ref[t:t + KEY_TILE, :].astype(F32).T.astype(BF16)
        vt_ref[C_VDIM:, :] = jnp.ones((vt_ref.shape[0] - C_VDIM, vt_ref.shape[1]), BF16)

    lam = (jnp.exp(jnp.sum(lam_ref[0:1, :] * lam_ref[1:2, :], axis=-1, keepdims=True))
           - jnp.exp(jnp.sum(lam_ref[2:3, :] * lam_ref[3:4, :], axis=-1, keepdims=True)) + lam_init)

    q = q_ref[...].astype(BF16)
    lane = lax.broadcasted_iota(jnp.int32, q.shape, 1)
    zero = jnp.zeros_like(q)
    qz = jnp.concatenate([jnp.where(lane < HEAD_DIM, q, zero), jnp.where(lane >= HEAD_DIM, q, zero)], axis=0)
    m = jnp.full((1, 2 * tq), -jnp.inf, F32)
    acc = jnp.zeros((vt_ref.shape[0], 2 * tq), F32)
    rb = 32

    def scores(i):
        k_ref, _, t = tiles[i]
        s = _dot_nt(k_ref[t:t + KEY_TILE, :].astype(BF16), qz)
        s_ref[i % 2] = s
        return jnp.max(s, axis=0, keepdims=True)

    top = scores(0)
    for i in range(len(tiles)):
        m_new = jnp.maximum(m, top)
        if i + 1 < len(tiles):
            top = scores(i + 1)
        corr = jnp.exp2(m - m_new)
        for r0 in range(0, KEY_TILE, rb):
            e_ref[i % 2, r0:r0 + rb, :] = jnp.exp2(s_ref[i % 2, r0:r0 + rb, :] - m_new).astype(BF16)
        acc = acc * corr + _dot(vt_ref[:, cols[i]], e_ref[i % 2])
        m = m_new
    l = acc[C_VDIM:C_VDIM + 1]
    acc = acc[:C_VDIM]
    o = acc[:, :tq] / l[:, :tq] - lam * (acc[:, tq:] / l[:, tq:])
    on = o * lax.rsqrt(jnp.mean(o * o, axis=0, keepdims=True) + SUBLN_EPS)
    o_ref[...] = (on.T * subln_ref[...] * (1.0 - lam_init)).astype(o_ref.dtype)


def _attn_c(pc, lam_vecs, subln, k_ctx, v_ctx, j, *, lam_init, n_seq, seq_len):
    n = pc.shape[0]
    tq = min(1024, seq_len)
    nq = seq_len // tq
    latent = k_ctx is not None
    in_specs = [pl.BlockSpec((4, HEAD_DIM), lambda b, h, i: (0, 0)),
                pl.BlockSpec((1, C_VDIM), lambda b, h, i: (0, 0)),
                pl.BlockSpec((tq, C_VDIM), lambda b, h, i: (b * nq + i, h))]
    args = [lam_vecs, subln, pc]
    own_k = pl.BlockSpec((seq_len, C_VDIM), lambda b, h, i: (b, C_HEADS + h))
    own_v = pl.BlockSpec((seq_len, C_VDIM), lambda b, h, i: (b, 2 * C_HEADS + h))
    if latent:
        n_ctx = k_ctx.shape[2]
        in_specs += [pl.BlockSpec((None, None, n_ctx, C_VDIM), lambda b, h, i: (b, j, 0, h))] * 2
        args += [k_ctx, v_ctx]
        n_lat = seq_len
    else:
        n_ctx, n_lat = seq_len, 0
    in_specs += [own_k, own_v]
    args += [pc, pc]
    key_tile = 512 if (n_ctx % 512 == 0 and n_lat % 512 == 0) else 256
    return pl.pallas_call(
        functools.partial(_attn_c_kernel, lam_init=lam_init, n_lat=n_lat, n_ctx=n_ctx),
        grid=(n_seq, C_HEADS, nq),
        in_specs=in_specs,
        out_specs=pl.BlockSpec((tq, C_VDIM), lambda b, h, i: (b * nq + i, h)),
        out_shape=jax.ShapeDtypeStruct((n, C_WIDTH), BF16),
        scratch_shapes=[pltpu.VMEM((C_VDIM + 16, n_ctx + n_lat), BF16),
                        pltpu.VMEM((2, key_tile, 2 * tq), F32),
                        pltpu.VMEM((2, key_tile, 2 * tq), BF16)],
        compiler_params=_params("parallel", "parallel", "arbitrary"),
        name="mixer_c_attention",
    )(*args)


def _pair_ones():
    idx = np.arange(LANES) // HEAD_DIM
    ones = (idx[:, None] == idx[None, :]).astype(np.float32)
    return jnp.asarray(np.concatenate([ones, ones], axis=0), dtype=BF16)


def _head_sum(x, ones2):
    out = []
    for p in range(PAIRS):
        hi, lo = _split(x[:, p * LANES:(p + 1) * LANES])
        out.append(_dot(jnp.concatenate([hi, lo], axis=1), ones2))
    return jnp.concatenate(out, axis=1)


def _rwkv_prep_kernel(pb_ref, hp_ref, hn_ref, mup_ref, mun_ref, w0_ref, w2_ref, a0_ref, a2_ref,
                      kk_ref, ka_ref, rk_ref, g2_ref, bd_ref,
                      r_out, v_out, kkn_out, g_out, bonus_out, lw_out, kd_out, ab_out, *, tiles_per_seq):
    i = pl.program_id(0)
    tm = pb_ref.shape[0]
    first = i % tiles_per_seq == 0
    last = i % tiles_per_seq == tiles_per_seq - 1
    pb = pb_ref[...]
    rowi = lax.broadcasted_iota(jnp.int32, (8, 1), 0)
    prev_row = jnp.where(first, 0.0, hp_ref[7:8, :])
    next_row = jnp.where(last, 0.0, hn_ref[0:1, :])
    prev = pltpu.roll(pb, 1, axis=0)
    prev = jnp.concatenate([jnp.where(rowi == 0, prev_row, prev[:8]), prev[8:]], axis=0)
    nxt = pltpu.roll(pb, tm - 1, axis=0)
    nxt = jnp.concatenate([nxt[:tm - 8], jnp.where(rowi == 7, next_row, nxt[tm - 8:])], axis=0)
    mup, mun = mup_ref[...], mun_ref[...]
    ps = pb * (1.0 - mup - mun) + mup * prev + mun * nxt

    o = 3 * B_WIDTH
    r = ps[:, 0:B_WIDTH]
    k = ps[:, B_WIDTH:2 * B_WIDTH]
    v = ps[:, 2 * B_WIDTH:o]
    w_dn = (ps[:, o:o + DECAY_LORA], ps[:, o + DECAY_LORA:o + 2 * DECAY_LORA])
    o += 2 * DECAY_LORA
    a_dn = (ps[:, o:o + AAA_LORA], ps[:, o + AAA_LORA:o + 2 * AAA_LORA])
    o += 2 * AAA_LORA
    gd = ps[:, o:o + GATE_LORA]

    ones2 = bd_ref[...]
    kk = k * kk_ref[...]
    kk = kk * jnp.minimum(lax.rsqrt(_head_sum(kk * kk, ones2)), 1e12)
    r_out[...] = r
    v_out[...] = v
    kkn_out[...] = kk
    g_out[...] = _mm3(jax.nn.sigmoid(gd), g2_ref[...])
    bonus = None
    for d in range(2):
        wl = w0_ref[d:d + 1, :] + _mm3(jnp.tanh(w_dn[d]), w2_ref[d])
        w_log = -jax.nn.softplus(-wl) - 0.5
        lw_out[d] = -jnp.exp(w_log)
        a = jax.nn.sigmoid(a0_ref[d:d + 1, :] + _mm3(a_dn[d], a2_ref[d]))
        k_d = k * (1.0 + (a - 1.0) * ka_ref[...])
        kd_out[d] = k_d
        ab_out[d] = a * kk
        bo = _head_sum(r * k_d * rk_ref[...], ones2) * v
        bonus = bo if bonus is None else bonus + bo
    bonus_out[...] = bonus


def _rwkv_prep(pb, W, j, *, seq_len):
    n = pb.shape[0]
    tm = 256
    tps = seq_len // tm
    nblk8 = n // 8
    row = lambda a: a.reshape(1, -1)
    res = _resident
    in_specs = [pl.BlockSpec((tm, B_COLS), lambda i: (i, 0)),
                pl.BlockSpec((8, B_COLS), lambda i: (jnp.maximum(i * (tm // 8) - 1, 0), 0)),
                pl.BlockSpec((8, B_COLS), lambda i: (jnp.minimum((i + 1) * (tm // 8), nblk8 - 1), 0)),
                res((1, B_COLS)), res((1, B_COLS)),
                res((2, B_WIDTH)), res((2, DECAY_LORA, B_WIDTH)),
                res((2, B_WIDTH)), res((2, AAA_LORA, B_WIDTH)),
                res((1, B_WIDTH)), res((1, B_WIDTH)), res((1, B_WIDTH)),
                res((GATE_LORA, B_WIDTH)), res((2 * LANES, LANES))]
    one = pl.BlockSpec((tm, B_WIDTH), lambda i: (i, 0))
    two = pl.BlockSpec((2, tm, B_WIDTH), lambda i: (0, i, 0))
    s1 = jax.ShapeDtypeStruct((n, B_WIDTH), F32)
    s2 = jax.ShapeDtypeStruct((2, n, B_WIDTH), F32)
    return pl.pallas_call(
        functools.partial(_rwkv_prep_kernel, tiles_per_seq=tps),
        grid=(n // tm,),
        in_specs=in_specs,
        out_specs=[one, one, one, one, one, two, two, two],
        out_shape=[s1, s1, s1, s1, s1, s2, s2, s2],
        compiler_params=_params("parallel"),
        name="rwkv_prep",
    )(pb, pb, pb, row(W["b_mu_prev"][j]), row(W["b_mu_next"][j]), W["b_w0"][j], W["b_w2"][j],
      W["b_a0"][j], W["b_a2"][j], row(W["b_k_k"][j]), row(W["b_k_a"][j]), row(W["b_r_k"][j]),
      W["b_g2"][j], _pair_ones())


def _bd(x):
    lane = lax.broadcasted_iota(jnp.int32, x.shape, 1)
    return jnp.concatenate([jnp.where(lane < HEAD_DIM, x, 0.0), jnp.where(lane >= HEAD_DIM, x, 0.0)], axis=0)


def _pair_diag(full):
    lane = lax.broadcasted_iota(jnp.int32, (HEAD_DIM, LANES), 1)
    return jnp.where(lane < HEAD_DIM, full[:HEAD_DIM, :], full[HEAD_DIM:, :])


def _split(x):
    hi = x.astype(BF16)
    return hi, (x - hi.astype(F32)).astype(BF16)


def _lhs3(a):
    ah, al = _split(a)
    return jnp.concatenate([ah, al, ah], axis=1)


def _rhs3(b, bd, axis):
    bh, bl = _split(b)
    if bd:
        bh, bl = _bd(bh), _bd(bl)
    return jnp.concatenate([bh, bh, bl], axis=axis)


def _mm3(a, b, bd=False):
    bs = b if isinstance(b, (list, tuple)) else [b]
    rhs = [_rhs3(x, bd, 0) for x in bs]
    return _dot(_lhs3(a), rhs[0] if len(rhs) == 1 else jnp.concatenate(rhs, axis=1))


def _mm3_nt(a, bs):
    rhs = [_rhs3(x, True, 1) for x in bs]
    return _dot_nt(_lhs3(a), rhs[0] if len(rhs) == 1 else jnp.concatenate(rhs, axis=0))


def _mm3_tn(a, b):
    (ah, al), (bh, bl) = _split(a), _split(b)
    return _dot_tn(jnp.concatenate([ah, al, ah], axis=0), jnp.concatenate([bh, bh, bl], axis=0))


STEP_CHUNKS = 4


def _rwkv_chunk_kernel(r_ref, v_ref, kk_ref, lw_ref, kd_ref, ab_ref, m_out, n_out, qt_out, pv_out):
    d = pl.program_id(0)
    row = lax.broadcasted_iota(jnp.int32, (CHUNK, LANES), 0)
    col = lax.broadcasted_iota(jnp.int32, (CHUNK, LANES), 1) % HEAD_DIM
    ahead = jnp.where(d == 0, row - col, col - row)
    strict = ahead > 0
    incl = ahead >= 0
    eye = (col == row).astype(F32)
    tri = jnp.where(incl[:, :CHUNK], 1.0, 0.0).astype(BF16)
    H = CHUNK
    units = [(g, p) for g in range(STEP_CHUNKS) for p in range(PAIRS)]
    at = [(slice(g * CHUNK, (g + 1) * CHUNK), slice(p * LANES, (p + 1) * LANES)) for g, p in units]
    st = [(g, slice(p * HEAD_DIM, (p + 1) * HEAD_DIM)) for g, p in units]

    lw = [lw_ref[t, ls] for t, ls in at]
    cum = []
    for x in lw:
        hi = x.astype(BF16)
        r1 = x - hi.astype(F32)
        mid = r1.astype(BF16)
        lo = (r1 - mid.astype(F32)).astype(BF16)
        y = _dot(tri, jnp.concatenate([hi, mid, lo], axis=1))
        cum.append((y[:, :LANES] + y[:, LANES:2 * LANES]) + y[:, 2 * LANES:])
    w_inv = [jnp.exp(-c) for c in cum]
    w_all = [jnp.exp(jnp.sum(x, axis=0, keepdims=True)) for x in lw]
    alpha = [jnp.exp(c - x) * kk_ref[t, ls] for c, x, (t, ls) in zip(cum, lw, at)]
    beta = [ab_ref[t, ls] * w for w, (t, ls) in zip(w_inv, at)]
    kappa = [kd_ref[t, ls] * w for w, (t, ls) in zip(w_inv, at)]
    rho = [jnp.exp(c) * r_ref[t, ls] for c, (t, ls) in zip(cum, at)]

    cross = [_mm3_nt(jnp.concatenate([a, r], axis=0), [b, k]) for a, r, b, k in zip(alpha, rho, beta, kappa)]
    l_ab = [jnp.where(strict, y[:H, :LANES], 0.0) for y in cross]
    l_rb = [jnp.where(incl, y[H:, :LANES], 0.0) for y in cross]
    l_ak = [jnp.where(strict, y[:H, LANES:], 0.0) for y in cross]
    l_rk = [jnp.where(incl, y[H:, LANES:], 0.0) for y in cross]

    pw = [-x for x in l_ab]
    t_inv = [eye + x for x in pw]
    pw = [_mm3(x, x, bd=True) for x in pw]
    for _ in range(int(math.log2(CHUNK)) - 2):
        both = [_mm3(jnp.concatenate([t, x], axis=0), x, bd=True) for t, x in zip(t_inv, pw)]
        t_inv = [t + y[:H] for t, y in zip(t_inv, both)]
        pw = [y[H:] for y in both]
    t_inv = [t + _mm3(t, x, bd=True) for t, x in zip(t_inv, pw)]

    xs = [_mm3(t, [a, x], bd=True) for t, a, x in zip(t_inv, alpha, l_ak)]
    xa = [x[:, :LANES] for x in xs]
    xk = [x[:, LANES:] for x in xs]
    lx = [_mm3(lb, [a, k], bd=True) for lb, a, k in zip(l_rb, xa, xk)]
    for u, (t, ls) in enumerate(at):
        qt_out[t, ls] = (rho[u] - lx[u][:, :LANES]).astype(qt_out.dtype)
    for u, (t, ls) in enumerate(at):
        pm = l_rk[u] - lx[u][:, LANES:]
        pv_out[t, ls] = _dot(pm.astype(BF16), _bd(v_ref[t, ls].astype(BF16)))
    g = [_mm3_tn(x, b) for x, b in zip(xs, beta)]
    for u, (c, rs) in enumerate(st):
        m_out[c, rs, :] = (eye - _pair_diag(g[u][:LANES])) * w_all[u]
    z = [k - _pair_diag(y[LANES:]) for k, y in zip(kappa, g)]
    for u, ((t, ls), (c, rs)) in enumerate(zip(at, st)):
        n_out[c, rs, :] = _pair_diag(_mm3_tn(v_ref[t, ls], z[u])) * w_all[u]


def _rwkv_chunks(r, v, kk, lw, kd, ab, *, n_seq, seq_len):
    n = r.shape[0]
    nc = seq_len // CHUNK
    ng = nc // STEP_CHUNKS
    tm = CHUNK * STEP_CHUNKS
    one = pl.BlockSpec((tm, B_WIDTH), lambda d, b, c: (b * ng + c, 0))
    two = pl.BlockSpec((None, tm, B_WIDTH), lambda d, b, c: (d, b * ng + c, 0))
    st = pl.BlockSpec((None, None, STEP_CHUNKS, PAIRS * HEAD_DIM, LANES), lambda d, b, c: (d, b, c, 0, 0))
    st_shape = jax.ShapeDtypeStruct((2, n_seq, nc, PAIRS * HEAD_DIM, LANES), F32)
    return pl.pallas_call(
        _rwkv_chunk_kernel,
        grid=(2, n_seq, ng),
        in_specs=[one, one, one, two, two, two],
        out_specs=[st, st, two, two],
        out_shape=[st_shape, st_shape, jax.ShapeDtypeStruct((2, n, B_WIDTH), BF16),
                   jax.ShapeDtypeStruct((2, n, B_WIDTH), F32)],
        compiler_params=_params("parallel", "parallel", "parallel"),
        name="rwkv_chunk_operators",
    )(r, v, kk, lw, kd, ab)


def _rwkv_scan_kernel(s0_ref, m_ref, n_ref, start_out, fin_out, s_scr):
    c = pl.program_id(2)
    SCAN_SEQS = s_scr.shape[0]

    @pl.when(c == 0)
    def _():
        s_scr[...] = s0_ref[...]

    for b in range(SCAN_SEQS):
        for p in range(PAIRS):
            rs = slice(p * HEAD_DIM, (p + 1) * HEAD_DIM)
            s = s_scr[b, rs, :]
            start_out[b, rs, :] = s
            s_scr[b, rs, :] = _mm3(s, m_ref[b, rs, :], bd=True) + n_ref[b, rs, :]

    @pl.when(c == pl.num_programs(2) - 1)
    def _():
        fin_out[...] = s_scr[...]


def _rwkv_scan(s0, m, nn):
    _, n_seq, nc, rows, _ = m.shape
    SCAN_SEQS = min(8, n_seq)
    chunk_of = lambda d, c: c + d * (nc - 1 - 2 * c)
    blk = pl.BlockSpec((None, SCAN_SEQS, None, rows, LANES), lambda d, g, c: (d, g, chunk_of(d, c), 0, 0))
    ends = pl.BlockSpec((None, SCAN_SEQS, rows, LANES), lambda d, g, c: (d, g, 0, 0))
    return pl.pallas_call(
        _rwkv_scan_kernel,
        grid=(2, n_seq // SCAN_SEQS, nc),
        in_specs=[ends, blk, blk],
        out_specs=[blk, ends],
        out_shape=[jax.ShapeDtypeStruct(m.shape, F32), jax.ShapeDtypeStruct((2, n_seq, rows, LANES), F32)],
        scratch_shapes=[pltpu.VMEM((SCAN_SEQS, rows, LANES), F32)],
        compiler_params=_params("parallel", "parallel", "arbitrary"),
        name="rwkv_state_scan",
    )(s0, m, nn)


OUT_CHUNKS = 4


def _rwkv_out_kernel(qf_ref, qb_ref, pf_ref, pb_ref, sf_ref, sb_ref, bonus_ref, g_ref, gw_ref, gb_ref, ones_ref,
                     o_ref, y_scr):
    for c in range(OUT_CHUNKS):
        tok = slice(c * CHUNK, (c + 1) * CHUNK)
        for p in range(PAIRS):
            ls = slice(p * LANES, (p + 1) * LANES)
            rs = slice(p * HEAD_DIM, (p + 1) * HEAD_DIM)
            y = pf_ref[tok, ls] + pb_ref[tok, ls]
            y = y + _dot_nt(qf_ref[tok, ls], _bd(sf_ref[c, rs, :].astype(BF16)))
            y = y + _dot_nt(qb_ref[tok, ls], _bd(sb_ref[c, rs, :].astype(BF16)))
            y_scr[tok, ls] = y
    y = y_scr[...]
    yc = y - _head_sum(y, ones_ref[...]) * (1.0 / HEAD_DIM)
    var = _head_sum(yc * yc, ones_ref[...]) * (1.0 / HEAD_DIM)
    yn = yc * lax.rsqrt(var + GN_EPS) * gw_ref[...] + gb_ref[...] + bonus_ref[...]
    o_ref[...] = (yn * g_ref[...]).astype(o_ref.dtype)


def _rwkv_out(qt, pv, starts, bonus, g, gn_w, gn_b, *, n_seq, seq_len):
    n = bonus.shape[0]
    ng = seq_len // (CHUNK * OUT_CHUNKS)
    tm = CHUNK * OUT_CHUNKS
    rows = PAIRS * HEAD_DIM
    tok = lambda d: pl.BlockSpec((None, tm, B_WIDTH), lambda b, c: (d, b * ng + c, 0))
    st = lambda d: pl.BlockSpec((None, None, OUT_CHUNKS, rows, LANES), lambda b, c: (d, b, c, 0, 0))
    one = pl.BlockSpec((tm, B_WIDTH), lambda b, c: (b * ng + c, 0))
    vec = pl.BlockSpec((1, B_WIDTH), lambda b, c: (0, 0))
    return pl.pallas_call(
        _rwkv_out_kernel,
        grid=(n_seq, ng),
        in_specs=[tok(0), tok(1), tok(0), tok(1), st(0), st(1), one, one, vec, vec,
                  pl.BlockSpec((2 * LANES, LANES), lambda b, c: (0, 0))],
        out_specs=one,
        out_shape=jax.ShapeDtypeStruct((n, B_WIDTH), BF16),
        scratch_shapes=[pltpu.VMEM((tm, B_WIDTH), F32)],
        compiler_params=_params("parallel", "parallel"),
        name="rwkv_output",
    )(qt, qt, pv, pv, starts, starts, bonus, g, gn_w.reshape(1, -1), gn_b.reshape(1, -1), _pair_ones())


def _to_pairs(s):
    lead = s.shape[:-3]
    s = s.reshape(*lead, PAIRS, 2, HEAD_DIM, HEAD_DIM)
    s = jnp.swapaxes(s, -3, -2)
    return s.reshape(*lead, PAIRS * HEAD_DIM, LANES)


def _from_pairs(s):
    lead = s.shape[:-2]
    s = s.reshape(*lead, PAIRS, HEAD_DIM, 2, HEAD_DIM)
    s = jnp.swapaxes(s, -3, -2)
    return s.reshape(*lead, B_HEADS, HEAD_DIM, HEAD_DIM)


def _rwkv_mixer(pb, s0, j, W, *, n_seq, seq_len):
    r, v, kk, g, bonus, lw, kd, ab = _rwkv_prep(pb, W, j, seq_len=seq_len)
    m, nn, qt, pv = _rwkv_chunks(r, v, kk, lw, kd, ab, n_seq=n_seq, seq_len=seq_len)
    starts, fin = _rwkv_scan(s0, m, nn)
    ob = _rwkv_out(qt, pv, starts, bonus, g, W["b_gn_w"][j], W["b_gn_b"][j], n_seq=n_seq, seq_len=seq_len)
    return ob, fin


def _forward(x, mod, caches, W, Wb, *, row0):
    n_seq, seq_len, _ = x.shape
    latent = caches is not None
    x = x.reshape(n_seq * seq_len, D_MODEL)
    rope = _rope_tables(seq_len) if latent else None
    kw = dict(seq_len=seq_len, row0=row0)
    norm = lambda l, s: W["norms"][l, s].reshape(1, D_MODEL)
    gf = W["final_norm"].reshape(1, D_MODEL)
    ak, av, bs, ck, cv = [], [], [], [], []
    for l in range(DEPTH):
        j = l // 2
        x = _ffn(x, mod[l], norm(l, 0), Wb["ffn_w1"][l, 0], Wb["ffn_w3"][l, 0], Wb["ffn_w2"][l, 0], gf, sub=0, **kw)
        if l % 2 == 0:
            pa, pb = _proj(x, mod[l], norm(l, 1), Wb["w_in_ab"][j], rope, sub=1, splits=(A_COLS, B_COLS),
                           dtypes=(BF16 if latent else F32, F32), n_rope=(A_Q + A_KV) if latent else 0,
                           n_q=A_Q, **kw)
            if latent:
                k_ctx = caches["a_k"].reshape(*caches["a_k"].shape[:3], A_KV)
                v_ctx = caches["a_v"].reshape(*caches["a_v"].shape[:3], A_KV)
                s0 = _to_pairs(jnp.swapaxes(caches["b_s"][:, j], 0, 1))
            else:
                k_ctx = v_ctx = None
                s0 = jnp.zeros((2, n_seq, PAIRS * HEAD_DIM, LANES), F32)
                ak.append(pa[:, A_Q:A_Q + A_KV].reshape(n_seq, seq_len, A_KV_HEADS, HEAD_DIM))
                av.append(pa[:, A_Q + A_KV:].reshape(n_seq, seq_len, A_KV_HEADS, HEAD_DIM))
            oa = _attn_a(pa, W["a_sink"][j], k_ctx, v_ctx, j, n_seq=n_seq, seq_len=seq_len)
            ob, fin = _rwkv_mixer(pb, s0, j, W, n_seq=n_seq, seq_len=seq_len)
            if not latent:
                bs.append(jnp.swapaxes(_from_pairs(fin), 0, 1))
            w_out = Wb["w_out_ab"][j]
            mix = dict(mix_acts=(oa, ob), mix_ws=(w_out[:A_Q], w_out[A_Q:]))
        else:
            (pc,) = _proj(x, mod[l], norm(l, 1), Wb["w_in_c"][j], rope, sub=1, splits=(3 * C_WIDTH,),
                          dtypes=(BF16 if latent else F32,), n_rope=2 * C_WIDTH if latent else 0,
                          n_q=C_WIDTH, **kw)
            if latent:
                k_ctx = caches["c_k"].reshape(*caches["c_k"].shape[:3], C_WIDTH)
                v_ctx = caches["c_v"].reshape(*caches["c_v"].shape[:3], C_WIDTH)
            else:
                k_ctx = v_ctx = None
                ck.append(pc[:, C_WIDTH:2 * C_WIDTH].reshape(n_seq, seq_len, C_HEADS, C_VDIM))
                cv.append(pc[:, 2 * C_WIDTH:].reshape(n_seq, seq_len, C_HEADS, C_VDIM))
            lam_vecs = jnp.stack([W["c_lq1"][j], W["c_lk1"][j], W["c_lq2"][j], W["c_lk2"][j]])
            oc = _attn_c(pc, lam_vecs, W["c_subln"][j].reshape(1, C_VDIM), k_ctx, v_ctx, j,
                         lam_init=0.8 - 0.6 * math.exp(-0.3 * l), n_seq=n_seq, seq_len=seq_len)
            mix = dict(mix_acts=(oc,), mix_ws=(Wb["w_out_c"][j],))
        x = _ffn(x, mod[l], norm(l, 2), Wb["ffn_w1"][l, 1], Wb["ffn_w3"][l, 1], Wb["ffn_w2"][l, 1], gf,
                 sub=2, final=(l == DEPTH - 1), **mix, **kw)
    return x.reshape(n_seq, seq_len, D_MODEL), (ak, av, bs, ck, cv)


def kernel(x_prompt, x_sample, c, c_ctx, cache_a_k, cache_a_v, state_b_wkv, cache_c_k, cache_c_v, norms, mod_w, mod_b, ffn_w1, ffn_w3, ffn_w2, w_in_ab, w_out_ab, a_sink, b_mu_prev, b_mu_next, b_w0, b_w2, b_a0, b_a2, b_k_k, b_k_a, b_r_k, b_g2, b_gn_w, b_gn_b, w_in_c, w_out_c, c_lq1, c_lk1, c_lq2, c_lk2, c_subln, final_norm):
    W = dict(norms=norms, a_sink=a_sink, b_mu_prev=b_mu_prev, b_mu_next=b_mu_next,
             b_w0=b_w0, b_w2=b_w2, b_a0=b_a0, b_a2=b_a2, b_k_k=b_k_k, b_k_a=b_k_a, b_r_k=b_r_k,
             b_g2=b_g2, b_gn_w=b_gn_w, b_gn_b=b_gn_b,
             c_lq1=c_lq1, c_lk1=c_lk1, c_lq2=c_lq2, c_lk2=c_lk2, c_subln=c_subln, final_norm=final_norm)
    Wb = dict(ffn_w1=ffn_w1.astype(BF16), ffn_w3=ffn_w3.astype(BF16), ffn_w2=ffn_w2.astype(BF16),
              w_in_ab=w_in_ab.astype(BF16), w_out_ab=w_out_ab.astype(BF16),
              w_in_c=w_in_c.astype(BF16), w_out_c=w_out_c.astype(BF16))
    n_lat = c.shape[0]
    assert 1 + n_lat <= MOD_ROWS
    cond = jnp.zeros((MOD_ROWS, D_MODEL), F32).at[0].set(c_ctx).at[1:1 + n_lat].set(c)
    mod = _modulation(cond, mod_w, mod_b)
    y_prompt, (ak, av, bs, ck, cv) = _forward(x_prompt, mod, None, W, Wb, row0=0)
    caches = dict(a_k=cache_a_k, a_v=cache_a_v, b_s=state_b_wkv, c_k=cache_c_k, c_v=cache_c_v)
    y_sample, _ = _forward(x_sample, mod, caches, W, Wb, row0=1)
    return (y_prompt, y_sample, jnp.stack(ak, axis=1), jnp.stack(av, axis=1), jnp.stack(bs, axis=1),
            jnp.stack(ck, axis=1), jnp.stack(cv, axis=1))
```

```python
import functools
import math

import numpy as np
import jax
import jax.numpy as jnp
from jax import lax
from jax.experimental import pallas as pl
from jax.experimental.pallas import tpu as pltpu

F32 = jnp.float32
BF16 = jnp.bfloat16
HIGHEST = lax.Precision.HIGHEST

D_MODEL = 1024
DEPTH = 4
GRID_W = 64
HEAD_DIM = 64
BLOCK = 128
A_HEADS = 8
A_KV_HEADS = 2
A_GROUP = A_HEADS // A_KV_HEADS
A_Q = A_HEADS * HEAD_DIM
A_KV = A_KV_HEADS * HEAD_DIM
A_COLS = A_Q + 2 * A_KV
B_HEADS = 8
B_WIDTH = B_HEADS * HEAD_DIM
DECAY_LORA = 64
AAA_LORA = 64
GATE_LORA = 128
B_COLS = 3 * B_WIDTH + 2 * DECAY_LORA + 2 * AAA_LORA + GATE_LORA
C_HEADS = 8
C_VDIM = 2 * HEAD_DIM
C_WIDTH = C_HEADS * C_VDIM
D_FF = 2816
N_MOD = 9
ROPE_THETA = 10000.0
NORM_EPS = 1e-6
GN_EPS = 64e-5
SUBLN_EPS = 1e-5

LANES = 128
MOD_ROWS = 16
CHUNK = 64
PAIRS = B_HEADS // 2
VMEM_LIMIT = 56 * 2 ** 20


def _params(*sem):
    return pltpu.CompilerParams(dimension_semantics=sem, vmem_limit_bytes=VMEM_LIMIT)


def _dot(a, b, precision=None):
    return jnp.dot(a, b, preferred_element_type=F32, precision=precision)


def _dot_nt(a, b, precision=None):
    return lax.dot_general(a, b, (((1,), (1,)), ((), ())), preferred_element_type=F32, precision=precision)


def _dot_tn(a, b, precision=None):
    return lax.dot_general(a, b, (((0,), (0,)), ((), ())), preferred_element_type=F32, precision=precision)


def _silu(x):
    return x * jax.nn.sigmoid(x)


def _modnorm(x, g, scale, shift):
    y = x * lax.rsqrt(jnp.mean(x * x, axis=-1, keepdims=True) + NORM_EPS)
    return (y * g) * (1.0 + scale) + shift


def _mod_kernel(c_ref, w_ref, b_ref, o_ref):
    o_ref[...] = _dot(_silu(c_ref[...]), w_ref[...], HIGHEST) + b_ref[...]


def _modulation(cond, mod_w, mod_b):
    n_col = N_MOD * D_MODEL
    tn = D_MODEL
    out = pl.pallas_call(
        _mod_kernel,
        grid=(DEPTH, n_col // tn),
        in_specs=[pl.BlockSpec((MOD_ROWS, D_MODEL), lambda l, j: (0, 0)),
                  pl.BlockSpec((None, D_MODEL, tn), lambda l, j: (l, 0, j)),
                  pl.BlockSpec((None, 1, tn), lambda l, j: (l, 0, j))],
        out_specs=pl.BlockSpec((None, MOD_ROWS, tn), lambda l, j: (l, 0, j)),
        out_shape=jax.ShapeDtypeStruct((DEPTH, MOD_ROWS, n_col), F32),
        compiler_params=_params("parallel", "parallel"),
        name="modulation",
    )(cond, mod_w, mod_b.reshape(DEPTH, 1, n_col))
    return out.reshape(DEPTH, MOD_ROWS, N_MOD, D_MODEL)


def _mod_spec(tm, seq_len, row0):
    if row0 == 0:
        return pl.BlockSpec((None, N_MOD, D_MODEL), lambda i: (0, 0, 0))
    return pl.BlockSpec((None, N_MOD, D_MODEL), lambda i: (row0 + (i * tm) // seq_len, 0, 0))


def _resident(shape):
    nd = len(shape)
    return pl.BlockSpec(shape, lambda i: (0,) * nd, pipeline_mode=pl.Buffered(1))


FF_TILE = 256


def _ffn_kernel(*refs, sub, final, n_mix):
    x_ref, mod_ref, g_ref, w1_ref, w3_ref, w2_ref, gf_ref = refs[:7]
    mix_refs = refs[7:7 + 2 * n_mix]
    o_ref, act_ref = refs[7 + 2 * n_mix:]
    x = x_ref[...]
    if n_mix:
        y = _dot(mix_refs[0][...], mix_refs[n_mix][...])
        for a_ref, w_ref in zip(mix_refs[1:n_mix], mix_refs[n_mix + 1:]):
            y = y + _dot(a_ref[...], w_ref[...])
        x = x + mod_ref[5:6, :] * y
    h = _modnorm(x, g_ref[...], mod_ref[3 * sub + 1:3 * sub + 2, :], mod_ref[3 * sub:3 * sub + 1, :]).astype(BF16)
    for j in range(D_FF // FF_TILE):
        cols = slice(j * FF_TILE, (j + 1) * FF_TILE)
        a = _dot(h, w1_ref[:, cols])
        b = _dot(h, w3_ref[:, cols])
        act_ref[:, cols] = (_silu(a) * b).astype(BF16)
    y = x + 0.5 * mod_ref[3 * sub + 2:3 * sub + 3, :] * _dot(act_ref[...], w2_ref[...])
    if final:
        y = y * lax.rsqrt(jnp.mean(y * y, axis=-1, keepdims=True) + NORM_EPS) * gf_ref[...]
    o_ref[...] = y


def _ffn(x, mod_l, g, w1, w3, w2, gf, *, sub, seq_len, row0, final=False, mix_acts=(), mix_ws=()):
    n = x.shape[0]
    tm = min(1024, seq_len)
    return pl.pallas_call(
        functools.partial(_ffn_kernel, sub=sub, final=final, n_mix=len(mix_acts)),
        grid=(n // tm,),
        in_specs=([pl.BlockSpec((tm, D_MODEL), lambda i: (i, 0)),
                   _mod_spec(tm, seq_len, row0),
                   _resident((1, D_MODEL)),
                   _resident((D_MODEL, D_FF)),
                   _resident((D_MODEL, D_FF)),
                   _resident((D_FF, D_MODEL)),
                   _resident((1, D_MODEL))]
                  + [pl.BlockSpec((tm, a.shape[1]), lambda i: (i, 0)) for a in mix_acts]
                  + [_resident(w.shape) for w in mix_ws]),
        out_specs=pl.BlockSpec((tm, D_MODEL), lambda i: (i, 0)),
        out_shape=jax.ShapeDtypeStruct((n, D_MODEL), F32),
        scratch_shapes=[pltpu.VMEM((tm, D_FF), BF16)],
        compiler_params=_params("parallel"),
        name="ffn",
    )(x, mod_l, g, w1, w3, w2, gf, *mix_acts, *mix_ws)


def _swap16(x):
    lane = lax.broadcasted_iota(jnp.int32, x.shape, 1)
    return jnp.where(lane % 32 < 16, pltpu.roll(x, LANES - 16, axis=1), pltpu.roll(x, 16, axis=1))


Q_SCALE = HEAD_DIM ** -0.5 * math.log2(math.e)


def _proj_kernel(*refs, sub, splits, n_rope, n_q):
    if n_rope:
        x_ref, mod_ref, g_ref, w_ref, cos_ref, sin_ref = refs[:6]
        o_refs = refs[6:]
    else:
        x_ref, mod_ref, g_ref, w_ref = refs[:4]
        o_refs = refs[4:]
    h = _modnorm(x_ref[...], g_ref[...], mod_ref[3 * sub + 1:3 * sub + 2, :], mod_ref[3 * sub:3 * sub + 1, :]).astype(BF16)
    start = 0
    for o_ref, width in zip(o_refs, splits):
        for c in range(0, width, 256):
            wd = min(256, width - c)
            y = _dot(h, w_ref[:, start + c:start + c + wd])
            for p in range(0, wd, LANES):
                yp = y[:, p:p + LANES]
                if start + c + p < n_q:
                    yp = yp * Q_SCALE
                if start + c + p < n_rope:
                    yp = yp * cos_ref[...] + _swap16(yp) * sin_ref[...]
                o_ref[:, c + p:c + p + LANES] = yp.astype(o_ref.dtype)
        start += width


def _proj(x, mod_l, g, w, rope, *, sub, splits, dtypes, n_rope, n_q, seq_len, row0):
    n = x.shape[0]
    tm = min(1024, seq_len)
    n_col = w.shape[1]
    in_specs = [pl.BlockSpec((tm, D_MODEL), lambda i: (i, 0)),
                _mod_spec(tm, seq_len, row0),
                _resident((1, D_MODEL)),
                _resident((D_MODEL, n_col))]
    args = [x, mod_l, g, w]
    if n_rope:
        tiles = seq_len // tm
        in_specs += [pl.BlockSpec((tm, LANES), lambda i: (i % tiles, 0))] * 2
        args += list(rope)
    return pl.pallas_call(
        functools.partial(_proj_kernel, sub=sub, splits=splits, n_rope=n_rope, n_q=n_q),
        grid=(n // tm,),
        in_specs=in_specs,
        out_specs=[pl.BlockSpec((tm, wd), lambda i: (i, 0)) for wd in splits],
        out_shape=[jax.ShapeDtypeStruct((n, wd), dt) for wd, dt in zip(splits, dtypes)],
        compiler_params=_params("parallel"),
        name="mixer_in_proj",
    )(*args)


def _rope_tables(n_tok):
    t = jnp.arange(n_tok, dtype=jnp.int32)
    row = (t // GRID_W).astype(F32)
    col = (t % GRID_W).astype(F32)
    n_freq = HEAD_DIM // 4
    inv = jnp.power(ROPE_THETA, -jnp.arange(n_freq, dtype=F32) / n_freq)
    ar, ac = row[:, None] * inv, col[:, None] * inv
    cos = jnp.concatenate([jnp.cos(ar), jnp.cos(ar), jnp.cos(ac), jnp.cos(ac)], axis=1)
    sin = jnp.concatenate([-jnp.sin(ar), jnp.sin(ar), -jnp.sin(ac), jnp.sin(ac)], axis=1)
    return jnp.tile(cos, (1, 2)), jnp.tile(sin, (1, 2))


A_SPLIT = 4


def _attn_a_kernel(*refs, windowed):
    if windowed:
        sink_ref, q_ref, kp_ref, kc_ref, kn_ref, vp_ref, vc_ref, vn_ref, kx_ref, vx_ref, o_ref, e_ref = refs
    else:
        sink_ref, q_ref, kx_ref, vx_ref, o_ref, e_ref = refs
    i = pl.program_id(1)
    nb = pl.num_programs(1)
    n_ctx = kx_ref.shape[0]
    lane = lax.broadcasted_iota(jnp.int32, (BLOCK, LANES), 1)

    q32 = q_ref[...].astype(F32)
    blocks = []
    for h in range(A_HEADS):
        g = h // A_GROUP
        col = q32[:, (h // 2) * LANES:(h // 2 + 1) * LANES]
        if h % 2 != g:
            col = pltpu.roll(col, HEAD_DIM, axis=1)
        blocks.append(jnp.where(lane // HEAD_DIM == g, col, 0.0).astype(BF16))
    qz = jnp.concatenate(blocks, axis=0)

    k_parts = [kx_ref[...].astype(BF16)]
    v_parts = [vx_ref[...].astype(BF16)]
    if windowed:
        k_parts += [r[...].astype(BF16) for r in (kp_ref, kc_ref, kn_ref)]
        v_parts += [r[...].astype(BF16) for r in (vp_ref, vc_ref, vn_ref)]
        row = lax.broadcasted_iota(jnp.int32, (BLOCK, BLOCK), 0)
        colk = lax.broadcasted_iota(jnp.int32, (BLOCK, BLOCK), 1)
        ok_p = jnp.logical_and(colk >= row, i > 0)
        ok_n = jnp.logical_and(colk <= row, i < nb - 1)
        zeros = jnp.zeros((BLOCK, BLOCK), F32)
        bias = jnp.concatenate([jnp.zeros((BLOCK, n_ctx), F32), jnp.where(ok_p, 0.0, -jnp.inf), zeros,
                                jnp.where(ok_n, 0.0, -jnp.inf)], axis=1)
    keys = jnp.concatenate(k_parts, axis=0) if windowed else k_parts[0]
    vals = jnp.concatenate(v_parts, axis=0) if windowed else v_parts[0]
    hpg = A_HEADS // A_SPLIT
    rows_g = hpg * BLOCK
    s_grp = [_dot_nt(qz[g * rows_g:(g + 1) * rows_g], keys) for g in range(A_SPLIT)]
    dens, o_grp = [], []
    for g in range(A_SPLIT):
        for hh in range(hpg):
            h = g * hpg + hh
            s = s_grp[g][hh * BLOCK:(hh + 1) * BLOCK]
            if windowed:
                s = s + bias
            sink = sink_ref[h] * math.log2(math.e)
            m = jnp.maximum(jnp.max(s, axis=-1, keepdims=True), sink)
            e = jnp.exp2(s - m)
            dens.append(jnp.sum(e, axis=-1, keepdims=True) + jnp.exp2(sink - m))
            e_ref[h * BLOCK:(h + 1) * BLOCK, :] = e.astype(BF16)
        o_grp.append(_dot(e_ref[g * rows_g:(g + 1) * rows_g, :], vals))
    for c in range(A_HEADS // 2):
        halves = []
        for h in (2 * c, 2 * c + 1):
            o = o_grp[h // hpg][(h % hpg) * BLOCK:(h % hpg + 1) * BLOCK] / dens[h]
            if h % 2 != h // A_GROUP:
                o = pltpu.roll(o, HEAD_DIM, axis=1)
            halves.append(o)
        o_ref[:, c * LANES:(c + 1) * LANES] = jnp.where(lane < HEAD_DIM, halves[0], halves[1]).astype(o_ref.dtype)


def _attn_a(pa, sink, k_ctx, v_ctx, j, *, n_seq, seq_len):
    n = pa.shape[0]
    nb = seq_len // BLOCK
    kcol, vcol = A_Q // A_KV, A_Q // A_KV + 1
    smem = pl.BlockSpec(memory_space=pltpu.SMEM)
    q_spec = pl.BlockSpec((BLOCK, A_Q), lambda b, i: (b * nb + i, 0))
    if k_ctx is not None:
        n_ctx = k_ctx.shape[2]

        def blk(colblk, off):
            return pl.BlockSpec((BLOCK, A_KV), lambda b, i: (b * nb + jnp.clip(i + off, 0, nb - 1), colblk))

        in_specs = [smem, q_spec, blk(kcol, -1), blk(kcol, 0), blk(kcol, 1), blk(vcol, -1), blk(vcol, 0), blk(vcol, 1),
                    pl.BlockSpec((None, None, n_ctx, A_KV), lambda b, i: (b, j, 0, 0)),
                    pl.BlockSpec((None, None, n_ctx, A_KV), lambda b, i: (b, j, 0, 0))]
        args = [sink, pa, pa, pa, pa, pa, pa, pa, k_ctx, v_ctx]
    else:
        in_specs = [smem, q_spec,
                    pl.BlockSpec((seq_len, A_KV), lambda b, i: (b, kcol)),
                    pl.BlockSpec((seq_len, A_KV), lambda b, i: (b, vcol))]
        args = [sink, pa, pa, pa]
        n_ctx = seq_len
    n_keys = n_ctx + (3 * BLOCK if k_ctx is not None else 0)
    return pl.pallas_call(
        functools.partial(_attn_a_kernel, windowed=k_ctx is not None),
        grid=(n_seq, nb),
        in_specs=in_specs,
        out_specs=pl.BlockSpec((BLOCK, A_Q), lambda b, i: (b * nb + i, 0)),
        out_shape=jax.ShapeDtypeStruct((n, A_Q), BF16),
        scratch_shapes=[pltpu.VMEM((A_HEADS * BLOCK, n_keys), BF16)],
        compiler_params=_params("parallel", "parallel"),
        name="mixer_a_attention",
    )(*args)


def _attn_c_kernel(*refs, lam_init, n_lat, n_ctx):
    lam_ref, subln_ref, q_ref = refs[:3]
    rest = list(refs[3:])
    kx_ref = vx_ref = kl_ref = vl_ref = None
    if n_ctx:
        kx_ref, vx_ref = rest[:2]
        rest = rest[2:]
    if n_lat:
        kl_ref, vl_ref = rest[:2]
        rest = rest[2:]
    o_ref, vt_ref, s_ref, e_ref = rest
    KEY_TILE = s_ref.shape[1]
    tiles = ([(kx_ref, vx_ref, t) for t in range(0, n_ctx, KEY_TILE)]
             + [(kl_ref, vl_ref, t) for t in range(0, n_lat, KEY_TILE)])
    tq = q_ref.shape[0]
    heads = vt_ref.shape[0]
    cols = [slice(i * KEY_TILE, (i + 1) * KEY_TILE) for i in range(len(tiles))]

    lam = (jnp.exp(jnp.sum(lam_ref[0:1, :] * lam_ref[1:2, :], axis=-1, keepdims=True))
           - jnp.exp(jnp.sum(lam_ref[2:3, :] * lam_ref[3:4, :], axis=-1, keepdims=True)) + lam_init)

    for hd in range(heads):
        ls = slice(hd * C_VDIM, (hd + 1) * C_VDIM)

        @pl.when(pl.program_id(2) == 0)
        def _():
            for i, (_, v_ref, t) in enumerate(tiles):
                vt_ref[hd, :C_VDIM, cols[i]] = v_ref[t:t + KEY_TILE, ls].astype(F32).T.astype(BF16)
            vt_ref[hd, C_VDIM:, :] = jnp.ones((vt_ref.shape[1] - C_VDIM, vt_ref.shape[2]), BF16)

        q = q_ref[:, ls].astype(BF16)
        lane = lax.broadcasted_iota(jnp.int32, q.shape, 1)
        zero = jnp.zeros_like(q)
        qz = jnp.concatenate([jnp.where(lane < HEAD_DIM, q, zero), jnp.where(lane >= HEAD_DIM, q, zero)], axis=0)
        m = jnp.full((1, 2 * tq), -jnp.inf, F32)
        acc = jnp.zeros((vt_ref.shape[1], 2 * tq), F32)
        rb = 32

        def scores(i):
            k_ref, _, t = tiles[i]
            s = _dot_nt(k_ref[t:t + KEY_TILE, ls].astype(BF16), qz)
            s_ref[i % 2] = s
            return jnp.max(s, axis=0, keepdims=True)

        top = scores(0)
        for i in range(len(tiles)):
            m_new = jnp.maximum(m, top)
            if i + 1 < len(tiles):
                top = scores(i + 1)
            corr = jnp.exp2(m - m_new)
            for r0 in range(0, KEY_TILE, rb):
                e_ref[i % 2, r0:r0 + rb, :] = jnp.exp2(s_ref[i % 2, r0:r0 + rb, :] - m_new).astype(BF16)
            acc = acc * corr + _dot(vt_ref[hd, :, cols[i]], e_ref[i % 2])
            m = m_new
        l = acc[C_VDIM:C_VDIM + 1]
        acc = acc[:C_VDIM]
        o = acc[:, :tq] / l[:, :tq] - lam * (acc[:, tq:] / l[:, tq:])
        on = o * lax.rsqrt(jnp.mean(o * o, axis=0, keepdims=True) + SUBLN_EPS)
        o_ref[:, ls] = (on.T * subln_ref[...] * (1.0 - lam_init)).astype(o_ref.dtype)


def _attn_c(pc, lam_vecs, subln, k_ctx, v_ctx, j, *, lam_init, n_seq, seq_len):
    n = pc.shape[0]
    tq = min(1024, seq_len)
    nq = seq_len // tq
    latent = k_ctx is not None
    hps = 1 if seq_len * seq_len > 2 ** 18 else C_HEADS
    hw = hps * C_VDIM
    hblocks = C_HEADS // hps
    in_specs = [pl.BlockSpec((4, HEAD_DIM), lambda b, h, i: (0, 0)),
                pl.BlockSpec((1, C_VDIM), lambda b, h, i: (0, 0)),
                pl.BlockSpec((tq, hw), lambda b, h, i: (b * nq + i, h))]
    args = [lam_vecs, subln, pc]
    own_k = pl.BlockSpec((seq_len, hw), lambda b, h, i: (b, hblocks + h))
    own_v = pl.BlockSpec((seq_len, hw), lambda b, h, i: (b, 2 * hblocks + h))
    if latent:
        n_ctx = k_ctx.shape[2]
        in_specs += [pl.BlockSpec((None, None, n_ctx, hw), lambda b, h, i: (b, j, 0, h))] * 2
        args += [k_ctx, v_ctx]
        n_lat = seq_len
    else:
        n_ctx, n_lat = seq_len, 0
    in_specs += [own_k, own_v]
    args += [pc, pc]
    key_tile = 512 if (n_ctx % 512 == 0 and n_lat % 512 == 0) else 256
    return pl.pallas_call(
        functools.partial(_attn_c_kernel, lam_init=lam_init, n_lat=n_lat, n_ctx=n_ctx),
        grid=(n_seq, hblocks, nq),
        in_specs=in_specs,
        out_specs=pl.BlockSpec((tq, hw), lambda b, h, i: (b * nq + i, h)),
        out_shape=jax.ShapeDtypeStruct((n, C_WIDTH), BF16),
        scratch_shapes=[pltpu.VMEM((hps, C_VDIM + 16, n_ctx + n_lat), BF16),
                        pltpu.VMEM((2, key_tile, 2 * tq), F32),
                        pltpu.VMEM((2, key_tile, 2 * tq), BF16)],
        compiler_params=_params("parallel", "parallel", "arbitrary"),
        name="mixer_c_attention",
    )(*args)


def _pair_ones():
    idx = np.arange(LANES) // HEAD_DIM
    ones = (idx[:, None] == idx[None, :]).astype(np.float32)
    return jnp.asarray(np.concatenate([ones, ones], axis=0), dtype=BF16)


def _head_sum(x, ones2):
    out = []
    for p in range(PAIRS):
        hi, lo = _split(x[:, p * LANES:(p + 1) * LANES])
        out.append(_dot(jnp.concatenate([hi, lo], axis=1), ones2))
    return jnp.concatenate(out, axis=1)


def _rwkv_prep_kernel(pb_ref, hp_ref, hn_ref, mup_ref, mun_ref, w0_ref, w2_ref, a0_ref, a2_ref,
                      kk_ref, ka_ref, rk_ref, g2_ref, bd_ref,
                      r_out, v_out, kkn_out, g_out, bonus_out, lw_out, kd_out, ab_out, *, tiles_per_seq):
    i = pl.program_id(0)
    tm = pb_ref.shape[0]
    first = i % tiles_per_seq == 0
    last = i % tiles_per_seq == tiles_per_seq - 1
    pb = pb_ref[...]
    rowi = lax.broadcasted_iota(jnp.int32, (8, 1), 0)
    prev_row = jnp.where(first, 0.0, hp_ref[7:8, :])
    next_row = jnp.where(last, 0.0, hn_ref[0:1, :])
    prev = pltpu.roll(pb, 1, axis=0)
    prev = jnp.concatenate([jnp.where(rowi == 0, prev_row, prev[:8]), prev[8:]], axis=0)
    nxt = pltpu.roll(pb, tm - 1, axis=0)
    nxt = jnp.concatenate([nxt[:tm - 8], jnp.where(rowi == 7, next_row, nxt[tm - 8:])], axis=0)
    mup, mun = mup_ref[...], mun_ref[...]
    ps = pb * (1.0 - mup - mun) + mup * prev + mun * nxt

    o = 3 * B_WIDTH
    r = ps[:, 0:B_WIDTH]
    k = ps[:, B_WIDTH:2 * B_WIDTH]
    v = ps[:, 2 * B_WIDTH:o]
    w_dn = (ps[:, o:o + DECAY_LORA], ps[:, o + DECAY_LORA:o + 2 * DECAY_LORA])
    o += 2 * DECAY_LORA
    a_dn = (ps[:, o:o + AAA_LORA], ps[:, o + AAA_LORA:o + 2 * AAA_LORA])
    o += 2 * AAA_LORA
    gd = ps[:, o:o + GATE_LORA]

    ones2 = bd_ref[...]
    kk = k * kk_ref[...]
    kk = kk * jnp.minimum(lax.rsqrt(_head_sum(kk * kk, ones2)), 1e12)
    r_out[...] = r
    v_out[...] = v
    kkn_out[...] = kk
    g_out[...] = _mm3(jax.nn.sigmoid(gd), g2_ref[...])
    bonus = None
    for d in range(2):
        wl = w0_ref[d:d + 1, :] + _mm3(jnp.tanh(w_dn[d]), w2_ref[d])
        w_log = -jax.nn.softplus(-wl) - 0.5
        lw_out[d] = -jnp.exp(w_log)
        a = jax.nn.sigmoid(a0_ref[d:d + 1, :] + _mm3(a_dn[d], a2_ref[d]))
        k_d = k * (1.0 + (a - 1.0) * ka_ref[...])
        kd_out[d] = k_d
        ab_out[d] = a * kk
        bo = _head_sum(r * k_d * rk_ref[...], ones2) * v
        bonus = bo if bonus is None else bonus + bo
    bonus_out[...] = bonus


def _rwkv_prep(pb, W, j, *, seq_len):
    n = pb.shape[0]
    tm = 256
    tps = seq_len // tm
    nblk8 = n // 8
    row = lambda a: a.reshape(1, -1)
    res = _resident
    in_specs = [pl.BlockSpec((tm, B_COLS), lambda i: (i, 0)),
                pl.BlockSpec((8, B_COLS), lambda i: (jnp.maximum(i * (tm // 8) - 1, 0), 0)),
                pl.BlockSpec((8, B_COLS), lambda i: (jnp.minimum((i + 1) * (tm // 8), nblk8 - 1), 0)),
                res((1, B_COLS)), res((1, B_COLS)),
                res((2, B_WIDTH)), res((2, DECAY_LORA, B_WIDTH)),
                res((2, B_WIDTH)), res((2, AAA_LORA, B_WIDTH)),
                res((1, B_WIDTH)), res((1, B_WIDTH)), res((1, B_WIDTH)),
                res((GATE_LORA, B_WIDTH)), res((2 * LANES, LANES))]
    one = pl.BlockSpec((tm, B_WIDTH), lambda i: (i, 0))
    two = pl.BlockSpec((2, tm, B_WIDTH), lambda i: (0, i, 0))
    s1 = jax.ShapeDtypeStruct((n, B_WIDTH), F32)
    s2 = jax.ShapeDtypeStruct((2, n, B_WIDTH), F32)
    return pl.pallas_call(
        functools.partial(_rwkv_prep_kernel, tiles_per_seq=tps),
        grid=(n // tm,),
        in_specs=in_specs,
        out_specs=[one, one, one, one, one, two, two, two],
        out_shape=[s1, s1, s1, s1, s1, s2, s2, s2],
        compiler_params=_params("parallel"),
        name="rwkv_prep",
    )(pb, pb, pb, row(W["b_mu_prev"][j]), row(W["b_mu_next"][j]), W["b_w0"][j], W["b_w2"][j],
      W["b_a0"][j], W["b_a2"][j], row(W["b_k_k"][j]), row(W["b_k_a"][j]), row(W["b_r_k"][j]),
      W["b_g2"][j], _pair_ones())


def _bd(x):
    lane = lax.broadcasted_iota(jnp.int32, x.shape, 1)
    return jnp.concatenate([jnp.where(lane // HEAD_DIM == h, x, 0.0) for h in range(x.shape[1] // HEAD_DIM)], axis=0)


def _pair_diag(full):
    lane = lax.broadcasted_iota(jnp.int32, (HEAD_DIM, LANES), 1)
    return jnp.where(lane < HEAD_DIM, full[:HEAD_DIM, :], full[HEAD_DIM:, :])


def _split(x):
    hi = x.astype(BF16)
    return hi, (x - hi.astype(F32)).astype(BF16)


def _lhs3(a):
    ah, al = _split(a)
    return jnp.concatenate([ah, al, ah], axis=1)


def _rhs3(b, bd, axis):
    bh, bl = _split(b)
    if bd:
        bh, bl = _bd(bh), _bd(bl)
    return jnp.concatenate([bh, bh, bl], axis=axis)


def _mm3(a, b, bd=False):
    bs = b if isinstance(b, (list, tuple)) else [b]
    rhs = [_rhs3(x, bd, 0) for x in bs]
    return _dot(_lhs3(a), rhs[0] if len(rhs) == 1 else jnp.concatenate(rhs, axis=1))


def _mm3_nt(a, bs):
    rhs = [_rhs3(x, True, 1) for x in bs]
    return _dot_nt(_lhs3(a), rhs[0] if len(rhs) == 1 else jnp.concatenate(rhs, axis=0))


def _mm3_tn(a, b):
    (ah, al), (bh, bl) = _split(a), _split(b)
    return _dot_tn(jnp.concatenate([ah, al, ah], axis=0), jnp.concatenate([bh, bh, bl], axis=0))


STEP_CHUNKS = 4


def _rwkv_chunk_kernel(r_ref, v_ref, kk_ref, lw_ref, kd_ref, ab_ref, m_out, n_out, qt_out, pv_out):
    d = pl.program_id(0)
    row = lax.broadcasted_iota(jnp.int32, (CHUNK, LANES), 0)
    col = lax.broadcasted_iota(jnp.int32, (CHUNK, LANES), 1) % HEAD_DIM
    ahead = jnp.where(d == 0, row - col, col - row)
    strict = ahead > 0
    incl = ahead >= 0
    eye = (col == row).astype(F32)
    tri = jnp.where(incl[:, :CHUNK], 1.0, 0.0).astype(BF16)
    H = CHUNK
    units = [(g, p) for g in range(STEP_CHUNKS) for p in range(PAIRS)]
    at = [(slice(g * CHUNK, (g + 1) * CHUNK), slice(p * LANES, (p + 1) * LANES)) for g, p in units]
    st = [(g, slice(p * HEAD_DIM, (p + 1) * HEAD_DIM)) for g, p in units]

    lw = [lw_ref[t, ls] for t, ls in at]
    cum = []
    for x in lw:
        hi = x.astype(BF16)
        r1 = x - hi.astype(F32)
        mid = r1.astype(BF16)
        lo = (r1 - mid.astype(F32)).astype(BF16)
        y = _dot(tri, jnp.concatenate([hi, mid, lo], axis=1))
        cum.append((y[:, :LANES] + y[:, LANES:2 * LANES]) + y[:, 2 * LANES:])
    w_inv = [jnp.exp(-c) for c in cum]
    w_all = [jnp.exp(jnp.sum(x, axis=0, keepdims=True)) for x in lw]
    alpha = [jnp.exp(c - x) * kk_ref[t, ls] for c, x, (t, ls) in zip(cum, lw, at)]
    beta = [ab_ref[t, ls] * w for w, (t, ls) in zip(w_inv, at)]
    kappa = [kd_ref[t, ls] * w for w, (t, ls) in zip(w_inv, at)]
    rho = [jnp.exp(c) * r_ref[t, ls] for c, (t, ls) in zip(cum, at)]

    cross = [_mm3_nt(jnp.concatenate([a, r], axis=0), [b, k]) for a, r, b, k in zip(alpha, rho, beta, kappa)]
    l_ab = [jnp.where(strict, y[:H, :LANES], 0.0) for y in cross]
    l_rb = [jnp.where(incl, y[H:, :LANES], 0.0) for y in cross]
    l_ak = [jnp.where(strict, y[:H, LANES:], 0.0) for y in cross]
    l_rk = [jnp.where(incl, y[H:, LANES:], 0.0) for y in cross]

    def same_block(size):
        return (row // size) == (col // size)

    pw = [jnp.where(same_block(8), -x, 0.0) for x in l_ab]
    t_inv = [eye + x for x in pw]
    pw = [_mm3(x, x, bd=True) for x in pw]
    both = [_mm3(jnp.concatenate([t, x], axis=0), x, bd=True) for t, x in zip(t_inv, pw)]
    t_inv = [t + y[:H] + _mm3(t + y[:H], y[H:], bd=True) for t, y in zip(t_inv, both)]
    for size in (8, 16, 32):
        merge = jnp.logical_and(same_block(2 * size), jnp.logical_not(same_block(size)))
        off = [_mm3(jnp.where(merge, x, 0.0), t, bd=True) for x, t in zip(l_ab, t_inv)]
        t_inv = [t - _mm3(t, x, bd=True) for t, x in zip(t_inv, off)]

    xs = [_mm3(t, [a, x], bd=True) for t, a, x in zip(t_inv, alpha, l_ak)]
    xa = [x[:, :LANES] for x in xs]
    xk = [x[:, LANES:] for x in xs]
    lx = [_mm3(lb, [a, k], bd=True) for lb, a, k in zip(l_rb, xa, xk)]
    for u, (t, ls) in enumerate(at):
        qt_out[t, ls] = (rho[u] - lx[u][:, :LANES]).astype(qt_out.dtype)
    for u, (t, ls) in enumerate(at):
        pm = l_rk[u] - lx[u][:, LANES:]
        pv_out[t, ls] = _dot(pm.astype(BF16), _bd(v_ref[t, ls].astype(BF16)))
    g = [_mm3_tn(x, b) for x, b in zip(xs, beta)]
    for u, (c, rs) in enumerate(st):
        m_out[c, rs, :] = (eye - _pair_diag(g[u][:LANES])) * w_all[u]
    z = [k - _pair_diag(y[LANES:]) for k, y in zip(kappa, g)]
    for u, ((t, ls), (c, rs)) in enumerate(zip(at, st)):
        n_out[c, rs, :] = _pair_diag(_mm3_tn(v_ref[t, ls], z[u])) * w_all[u]


def _rwkv_chunks(r, v, kk, lw, kd, ab, *, n_seq, seq_len):
    n = r.shape[0]
    nc = seq_len // CHUNK
    ng = nc // STEP_CHUNKS
    tm = CHUNK * STEP_CHUNKS
    one = pl.BlockSpec((tm, B_WIDTH), lambda d, b, c: (b * ng + c, 0))
    two = pl.BlockSpec((None, tm, B_WIDTH), lambda d, b, c: (d, b * ng + c, 0))
    st = pl.BlockSpec((None, None, STEP_CHUNKS, PAIRS * HEAD_DIM, LANES), lambda d, b, c: (d, b, c, 0, 0))
    st_shape = jax.ShapeDtypeStruct((2, n_seq, nc, PAIRS * HEAD_DIM, LANES), F32)
    return pl.pallas_call(
        _rwkv_chunk_kernel,
        grid=(2, n_seq, ng),
        in_specs=[one, one, one, two, two, two],
        out_specs=[st, st, two, two],
        out_shape=[st_shape, st_shape, jax.ShapeDtypeStruct((2, n, B_WIDTH), BF16),
                   jax.ShapeDtypeStruct((2, n, B_WIDTH), F32)],
        compiler_params=_params("parallel", "parallel", "parallel"),
        name="rwkv_chunk_operators",
    )(r, v, kk, lw, kd, ab)


def _rwkv_scan_kernel(s0_ref, m_ref, n_ref, start_out, fin_out, s_scr):
    c = pl.program_id(2)
    SCAN_SEQS = s_scr.shape[0]

    @pl.when(c == 0)
    def _():
        s_scr[...] = s0_ref[...]

    for b in range(SCAN_SEQS):
        for p in range(PAIRS):
            rs = slice(p * HEAD_DIM, (p + 1) * HEAD_DIM)
            s = s_scr[b, rs, :]
            start_out[b, rs, :] = s.astype(start_out.dtype)
            s_scr[b, rs, :] = _mm3(s, m_ref[b, rs, :], bd=True) + n_ref[b, rs, :]

    @pl.when(c == pl.num_programs(2) - 1)
    def _():
        fin_out[...] = s_scr[...]


def _rwkv_scan(s0, m, nn):
    _, n_seq, nc, rows, _ = m.shape
    SCAN_SEQS = min(8, n_seq)
    chunk_of = lambda d, c: c + d * (nc - 1 - 2 * c)
    blk = pl.BlockSpec((None, SCAN_SEQS, None, rows, LANES), lambda d, g, c: (d, g, chunk_of(d, c), 0, 0))
    ends = pl.BlockSpec((None, SCAN_SEQS, rows, LANES), lambda d, g, c: (d, g, 0, 0))
    return pl.pallas_call(
        _rwkv_scan_kernel,
        grid=(2, n_seq // SCAN_SEQS, nc),
        in_specs=[ends, blk, blk],
        out_specs=[blk, ends],
        out_shape=[jax.ShapeDtypeStruct(m.shape, BF16), jax.ShapeDtypeStruct((2, n_seq, rows, LANES), F32)],
        scratch_shapes=[pltpu.VMEM((SCAN_SEQS, rows, LANES), F32)],
        compiler_params=_params("parallel", "parallel", "arbitrary"),
        name="rwkv_state_scan",
    )(s0, m, nn)


OUT_CHUNKS = 4


def _rwkv_out_kernel(qf_ref, qb_ref, pf_ref, pb_ref, sf_ref, sb_ref, bonus_ref, g_ref, gw_ref, gb_ref, ones_ref,
                     o_ref, y_scr):
    for c in range(OUT_CHUNKS):
        tok = slice(c * CHUNK, (c + 1) * CHUNK)
        for p in range(PAIRS):
            ls = slice(p * LANES, (p + 1) * LANES)
            rs = slice(p * HEAD_DIM, (p + 1) * HEAD_DIM)
            y = pf_ref[tok, ls] + pb_ref[tok, ls]
            y = y + _dot_nt(qf_ref[tok, ls], _bd(sf_ref[c, rs, :].astype(BF16)))
            y = y + _dot_nt(qb_ref[tok, ls], _bd(sb_ref[c, rs, :].astype(BF16)))
            y_scr[tok, ls] = y
    y = y_scr[...]
    yc = y - _head_sum(y, ones_ref[...]) * (1.0 / HEAD_DIM)
    var = _head_sum(yc * yc, ones_ref[...]) * (1.0 / HEAD_DIM)
    yn = yc * lax.rsqrt(var + GN_EPS) * gw_ref[...] + gb_ref[...] + bonus_ref[...]
    o_ref[...] = (yn * g_ref[...]).astype(o_ref.dtype)


def _rwkv_out(qt, pv, starts, bonus, g, gn_w, gn_b, *, n_seq, seq_len):
    n = bonus.shape[0]
    ng = seq_len // (CHUNK * OUT_CHUNKS)
    tm = CHUNK * OUT_CHUNKS
    rows = PAIRS * HEAD_DIM
    tok = lambda d: pl.BlockSpec((None, tm, B_WIDTH), lambda b, c: (d, b * ng + c, 0))
    st = lambda d: pl.BlockSpec((None, None, OUT_CHUNKS, rows, LANES), lambda b, c: (d, b, c, 0, 0))
    one = pl.BlockSpec((tm, B_WIDTH), lambda b, c: (b * ng + c, 0))
    vec = pl.BlockSpec((1, B_WIDTH), lambda b, c: (0, 0))
    return pl.pallas_call(
        _rwkv_out_kernel,
        grid=(n_seq, ng),
        in_specs=[tok(0), tok(1), tok(0), tok(1), st(0), st(1), one, one, vec, vec,
                  pl.BlockSpec((2 * LANES, LANES), lambda b, c: (0, 0))],
        out_specs=one,
        out_shape=jax.ShapeDtypeStruct((n, B_WIDTH), BF16),
        scratch_shapes=[pltpu.VMEM((tm, B_WIDTH), F32)],
        compiler_params=_params("parallel", "parallel"),
        name="rwkv_output",
    )(qt, qt, pv, pv, starts, starts, bonus, g, gn_w.reshape(1, -1), gn_b.reshape(1, -1), _pair_ones())


def _to_pairs(s):
    lead = s.shape[:-3]
    s = s.reshape(*lead, PAIRS, 2, HEAD_DIM, HEAD_DIM)
    s = jnp.swapaxes(s, -3, -2)
    return s.reshape(*lead, PAIRS * HEAD_DIM, LANES)


def _from_pairs(s):
    lead = s.shape[:-2]
    s = s.reshape(*lead, PAIRS, HEAD_DIM, 2, HEAD_DIM)
    s = jnp.swapaxes(s, -3, -2)
    return s.reshape(*lead, B_HEADS, HEAD_DIM, HEAD_DIM)


def _rwkv_mixer(pb, s0, j, W, *, n_seq, seq_len):
    r, v, kk, g, bonus, lw, kd, ab = _rwkv_prep(pb, W, j, seq_len=seq_len)
    m, nn, qt, pv = _rwkv_chunks(r, v, kk, lw, kd, ab, n_seq=n_seq, seq_len=seq_len)
    starts, fin = _rwkv_scan(s0, m, nn)
    ob = _rwkv_out(qt, pv, starts, bonus, g, W["b_gn_w"][j], W["b_gn_b"][j], n_seq=n_seq, seq_len=seq_len)
    return ob, fin


def _forward(x, mod, caches, W, Wb, *, row0):
    n_seq, seq_len, _ = x.shape
    latent = caches is not None
    x = x.reshape(n_seq * seq_len, D_MODEL)
    rope = _rope_tables(seq_len) if latent else None
    kw = dict(seq_len=seq_len, row0=row0)
    norm = lambda l, s: W["norms"][l, s].reshape(1, D_MODEL)
    gf = W["final_norm"].reshape(1, D_MODEL)
    ak, av, bs, ck, cv = [], [], [], [], []
    for l in range(DEPTH):
        j = l // 2
        x = _ffn(x, mod[l], norm(l, 0), Wb["ffn_w1"][l, 0], Wb["ffn_w3"][l, 0], Wb["ffn_w2"][l, 0], gf, sub=0, **kw)
        if l % 2 == 0:
            pa, pb = _proj(x, mod[l], norm(l, 1), Wb["w_in_ab"][j], rope, sub=1, splits=(A_COLS, B_COLS),
                           dtypes=(BF16 if latent else F32, F32), n_rope=(A_Q + A_KV) if latent else 0,
                           n_q=A_Q, **kw)
            if latent:
                k_ctx = caches["a_k"].reshape(*caches["a_k"].shape[:3], A_KV)
                v_ctx = caches["a_v"].reshape(*caches["a_v"].shape[:3], A_KV)
                s0 = _to_pairs(jnp.swapaxes(caches["b_s"][:, j], 0, 1))
            else:
                k_ctx = v_ctx = None
                s0 = jnp.zeros((2, n_seq, PAIRS * HEAD_DIM, LANES), F32)
                ak.append(pa[:, A_Q:A_Q + A_KV].reshape(n_seq, seq_len, A_KV_HEADS, HEAD_DIM))
                av.append(pa[:, A_Q + A_KV:].reshape(n_seq, seq_len, A_KV_HEADS, HEAD_DIM))
            oa = _attn_a(pa, W["a_sink"][j], k_ctx, v_ctx, j, n_seq=n_seq, seq_len=seq_len)
            ob, fin = _rwkv_mixer(pb, s0, j, W, n_seq=n_seq, seq_len=seq_len)
            if not latent:
                bs.append(jnp.swapaxes(_from_pairs(fin), 0, 1))
            w_out = Wb["w_out_ab"][j]
            mix = dict(mix_acts=(oa, ob), mix_ws=(w_out[:A_Q], w_out[A_Q:]))
        else:
            (pc,) = _proj(x, mod[l], norm(l, 1), Wb["w_in_c"][j], rope, sub=1, splits=(3 * C_WIDTH,),
                          dtypes=(BF16 if latent else F32,), n_rope=2 * C_WIDTH if latent else 0,
                          n_q=C_WIDTH, **kw)
            if latent:
                k_ctx = caches["c_k"].reshape(*caches["c_k"].shape[:3], C_WIDTH)
                v_ctx = caches["c_v"].reshape(*caches["c_v"].shape[:3], C_WIDTH)
            else:
                k_ctx = v_ctx = None
                ck.append(pc[:, C_WIDTH:2 * C_WIDTH].reshape(n_seq, seq_len, C_HEADS, C_VDIM))
                cv.append(pc[:, 2 * C_WIDTH:].reshape(n_seq, seq_len, C_HEADS, C_VDIM))
            lam_vecs = jnp.stack([W["c_lq1"][j], W["c_lk1"][j], W["c_lq2"][j], W["c_lk2"][j]])
            oc = _attn_c(pc, lam_vecs, W["c_subln"][j].reshape(1, C_VDIM), k_ctx, v_ctx, j,
                         lam_init=0.8 - 0.6 * math.exp(-0.3 * l), n_seq=n_seq, seq_len=seq_len)
            mix = dict(mix_acts=(oc,), mix_ws=(Wb["w_out_c"][j],))
        x = _ffn(x, mod[l], norm(l, 2), Wb["ffn_w1"][l, 1], Wb["ffn_w3"][l, 1], Wb["ffn_w2"][l, 1], gf,
                 sub=2, final=(l == DEPTH - 1), **mix, **kw)
    return x.reshape(n_seq, seq_len, D_MODEL), (ak, av, bs, ck, cv)


def kernel(x_prompt, x_sample, c, c_ctx, cache_a_k, cache_a_v, state_b_wkv, cache_c_k, cache_c_v, norms, mod_w, mod_b, ffn_w1, ffn_w3, ffn_w2, w_in_ab, w_out_ab, a_sink, b_mu_prev, b_mu_next, b_w0, b_w2, b_a0, b_a2, b_k_k, b_k_a, b_r_k, b_g2, b_gn_w, b_gn_b, w_in_c, w_out_c, c_lq1, c_lk1, c_lq2, c_lk2, c_subln, final_norm):
    W = dict(norms=norms, a_sink=a_sink, b_mu_prev=b_mu_prev, b_mu_next=b_mu_next,
             b_w0=b_w0, b_w2=b_w2, b_a0=b_a0, b_a2=b_a2, b_k_k=b_k_k, b_k_a=b_k_a, b_r_k=b_r_k,
             b_g2=b_g2, b_gn_w=b_gn_w, b_gn_b=b_gn_b,
             c_lq1=c_lq1, c_lk1=c_lk1, c_lq2=c_lq2, c_lk2=c_lk2, c_subln=c_subln, final_norm=final_norm)
    Wb = dict(ffn_w1=ffn_w1.astype(BF16), ffn_w3=ffn_w3.astype(BF16), ffn_w2=ffn_w2.astype(BF16),
              w_in_ab=w_in_ab.astype(BF16), w_out_ab=w_out_ab.astype(BF16),
              w_in_c=w_in_c.astype(BF16), w_out_c=w_out_c.astype(BF16))
    n_lat = c.shape[0]
    assert 1 + n_lat <= MOD_ROWS
    cond = jnp.zeros((MOD_ROWS, D_MODEL), F32).at[0].set(c_ctx).at[1:1 + n_lat].set(c)
    mod = _modulation(cond, mod_w, mod_b)
    y_prompt, (ak, av, bs, ck, cv) = _forward(x_prompt, mod, None, W, Wb, row0=0)
    caches = dict(a_k=cache_a_k, a_v=cache_a_v, b_s=state_b_wkv, c_k=cache_c_k, c_v=cache_c_v)
    y_sample, _ = _forward(x_sample, mod, caches, W, Wb, row0=1)
    return (y_prompt, y_sample, jnp.stack(ak, axis=1), jnp.stack(av, axis=1), jnp.stack(bs, axis=1),
            jnp.stack(ck, axis=1), jnp.stack(cv, axis=1))
```
